```python
import jax, jax.numpy as jnp
from jax import lax
import numpy as np

D_MODEL = 1024
BATCH = 8
SEQ = 2048
DEPTH = 4
DEC_BATCH = 32
DEC_SEQ = 1
PAST_LEN = 16384
PAGE_SIZE = 128

N_AB = (DEPTH + 1) // 2
N_C = DEPTH // 2
EPS = 1e-6
D_A = D_MODEL // 2
CONV_W = 31
H_B = 4
D_BV = D_MODEL // 2
D_BK = D_BV // 2
DK_B = D_BK // H_B
DV_B = D_BV // H_B
GATE_RANK = 16
GATE_TAU = 16.0
GLA_CHUNK = 64
D_IN_AB = 2 * D_A + 2 * D_BK + 2 * D_BV + GATE_RANK
H_C = 16
D_NOPE = 64
D_ROPE = 32
D_V = 64
R_Q = 384
R_KV = 256
D_IN_C = R_Q + R_KV + D_ROPE
ROPE_BASE = 10000.0
MLA_SCALE = (D_NOPE + D_ROPE) ** -0.5
Q_BLOCK = 128
H_P = 8
N_KEYS = 128
N_EXPERTS = N_KEYS * N_KEYS
D_KEY = 256
D_HALF = D_KEY // 2
TOPK = 16
PEER_BLOCK = 128

kernel_name = 'hybrid_conv_gla_mla_peer_step'


def rms_norm(x, g):
    xf = x.astype(jnp.float32)
    y = xf * lax.rsqrt(jnp.mean(xf * xf, axis=-1, keepdims=True) + EPS)
    return (y * g.astype(jnp.float32)).astype(x.dtype)


def layer_norm(x, g, b):
    xf = x.astype(jnp.float32)
    mu = jnp.mean(xf, axis=-1, keepdims=True)
    xc = xf - mu
    y = xc * lax.rsqrt(jnp.mean(xc * xc, axis=-1, keepdims=True) + EPS)
    return (y * g.astype(jnp.float32) + b.astype(jnp.float32)).astype(x.dtype)


def rope(x, pos):
    half = D_ROPE // 2
    inv = jnp.power(ROPE_BASE, -jnp.arange(half, dtype=jnp.float32) / half)
    ang = pos.astype(jnp.float32)[:, None] * inv[None, :]
    shape = (ang.shape[0],) + (1,) * (x.ndim - 3) + (half,)
    cos = jnp.cos(ang).reshape(shape)
    sin = jnp.sin(ang).reshape(shape)
    xf = x.astype(jnp.float32)
    x1, x2 = xf[..., :half], xf[..., half:]
    return jnp.concatenate([x1 * cos - x2 * sin, x1 * sin + x2 * cos], axis=-1).astype(x.dtype)


def gla_scan(q, k, v, log_a, s0):
    B, T, H, _ = q.shape
    DV = v.shape[-1]
    c = min(GLA_CHUNK, T)
    nc = -(-T // c)
    pad = nc * c - T

    def chunks(a):
        a = jnp.pad(a.astype(jnp.float32), ((0, 0), (0, pad), (0, 0), (0, 0)))
        return jnp.moveaxis(a.reshape(B, nc, c, H, a.shape[-1]), 1, 0)

    idx = jnp.arange(c)
    causal = (idx[:, None] >= idx[None, :])[None, :, :, None, None]

    def step(s, inp):
        qc, kc, vc, gc = inp
        b = jnp.cumsum(gc, axis=1)
        o_inter = jnp.einsum('bihk,bhkv->bihv', qc * jnp.exp(b), s)
        decay = jnp.exp(jnp.where(causal, b[:, :, None] - b[:, None, :], -jnp.inf))
        attn = jnp.einsum('bihk,bijhk->bhij', qc, decay * kc[:, None])
        o_intra = jnp.einsum('bhij,bjhv->bihv', attn, vc)
        b_end = b[:, -1]
        s_new = jnp.exp(b_end)[..., None] * s + jnp.einsum(
            'bjhk,bjhv->bhkv', kc * jnp.exp(b_end[:, None] - b), vc)
        return s_new, o_inter + o_intra

    s_fin, o = lax.scan(step, s0.astype(jnp.float32),
                        (chunks(q), chunks(k), chunks(v), chunks(log_a)))
    o = jnp.moveaxis(o, 0, 1).reshape(B, nc * c, H, DV)[:, :T]
    return o, s_fin


def even_layer(x, conv_buf, s0, norm_g, w_in, conv_w, conv_b, ln_g, ln_b,
               w_gate, b_gate, head_g, w_out):
    B, T, _ = x.shape
    z = rms_norm(x, norm_g) @ w_in
    offs = [D_A, 2 * D_A, 2 * D_A + D_BK, 2 * D_A + 2 * D_BK,
            2 * D_A + 2 * D_BK + D_BV, 2 * D_A + 2 * D_BK + 2 * D_BV]
    a_val, a_gate, q, k, v, r, g_low = jnp.split(z, offs, axis=-1)
    u = a_val * jax.nn.sigmoid(a_gate)
    ubuf = jnp.concatenate([conv_buf.astype(u.dtype), u], axis=1)
    c = lax.conv_general_dilated(ubuf, conv_w.astype(u.dtype)[:, None, :], (1,), 'VALID',
                                 dimension_numbers=('NWC', 'WIO', 'NWC'),
                                 feature_group_count=D_A) + conv_b
    a_out = jax.nn.silu(layer_norm(c, ln_g, ln_b))
    new_buf = ubuf[:, ubuf.shape[1] - (CONV_W - 1):]
    log_a = jax.nn.log_sigmoid((g_low @ w_gate + b_gate).astype(jnp.float32)) / GATE_TAU
    o, s_new = gla_scan(q.reshape(B, T, H_B, DK_B) * DK_B ** -0.5,
                        k.reshape(B, T, H_B, DK_B),
                        v.reshape(B, T, H_B, DV_B),
                        log_a.reshape(B, T, H_B, DK_B), s0)
    b_out = rms_norm(o, head_g).astype(x.dtype).reshape(B, T, D_BV) * jax.nn.silu(r)
    y = jnp.concatenate([a_out, b_out], axis=-1) @ w_out
    return x + y, new_buf, s_new.astype(x.dtype)


def mla_attend(q_lat, q_pe, ckv, kpe, ckv_past, kpe_past):
    B, T, H, R = q_lat.shape
    qb = min(Q_BLOCK, T)
    nb = -(-T // qb)
    pad = nb * qb - T

    def blocks(a):
        a = jnp.pad(a, ((0, 0), (0, pad)) + ((0, 0),) * (a.ndim - 2))
        return jnp.moveaxis(a.reshape((B, nb, qb) + a.shape[2:]), 1, 0)

    starts = jnp.arange(nb, dtype=jnp.int32) * qb
    kidx = jnp.arange(T, dtype=jnp.int32)

    def block(args):
        ql, qp, start = args
        qidx = start + jnp.arange(qb, dtype=jnp.int32)
        s = (jnp.einsum('bqhr,bkr->bhqk', ql, ckv)
             + jnp.einsum('bqhd,bkd->bhqk', qp, kpe)).astype(jnp.float32) * MLA_SCALE
        s = jnp.where(kidx[None, :] <= qidx[:, None], s, -jnp.inf)
        if ckv_past is None:
            p = jax.nn.softmax(s, axis=-1).astype(ckv.dtype)
            return jnp.einsum('bhqk,bkr->bqhr', p, ckv)
        sp = (jnp.einsum('bqhr,bkr->bhqk', ql, ckv_past)
              + jnp.einsum('bqhd,bkd->bhqk', qp, kpe_past)).astype(jnp.float32) * MLA_SCALE
        p = jax.nn.softmax(jnp.concatenate([sp, s], axis=-1), axis=-1).astype(ckv.dtype)
        n_past = ckv_past.shape[1]
        return (jnp.einsum('bhqk,bkr->bqhr', p[..., :n_past], ckv_past)
                + jnp.einsum('bhqk,bkr->bqhr', p[..., n_past:], ckv))

    o = lax.map(block, (blocks(q_lat), blocks(q_pe), starts))
    return jnp.moveaxis(o, 0, 1).reshape(B, nb * qb, H, R)[:, :T]


def odd_layer(x, pos, ckv_past, kpe_past, norm_g, w_in, q_norm_g, kv_norm_g,
              w_q_b, w_uk, w_uv, w_out):
    B, T, _ = x.shape
    z = rms_norm(x, norm_g) @ w_in
    c_q, c_kv, k_pe = jnp.split(z, [R_Q, R_Q + R_KV], axis=-1)
    q = (rms_norm(c_q, q_norm_g) @ w_q_b).reshape(B, T, H_C, D_NOPE + D_ROPE)
    q_nope = q[..., :D_NOPE]
    q_pe = rope(q[..., D_NOPE:], pos)
    ckv = rms_norm(c_kv, kv_norm_g)
    kpe = rope(k_pe, pos)
    q_lat = jnp.einsum('bthn,rhn->bthr', q_nope, w_uk)
    o_lat = mla_attend(q_lat, q_pe, ckv, kpe, ckv_past, kpe_past)
    o = jnp.einsum('bthr,rhv->bthv', o_lat, w_uv).reshape(B, T, H_C * D_V)
    return x + o @ w_out, ckv, kpe


def peer(h, w_q, sub_keys, u_tab, v_tab):
    lead = h.shape[:-1]
    xf = h.reshape(-1, D_MODEL)
    n = xf.shape[0]
    q = (xf @ w_q).reshape(n, H_P, 2, D_HALF)
    s = jnp.einsum('nhcd,hckd->nhck', q, sub_keys).astype(jnp.float32)
    s_top, i_top = lax.top_k(s, TOPK)
    cand = (s_top[:, :, 0, :, None] + s_top[:, :, 1, None, :]).reshape(n, H_P, TOPK * TOPK)
    c_top, c_idx = lax.top_k(cand, TOPK)
    i1 = jnp.take_along_axis(i_top[:, :, 0], c_idx // TOPK, axis=-1)
    i2 = jnp.take_along_axis(i_top[:, :, 1], c_idx % TOPK, axis=-1)
    ids = i1 * N_KEYS + i2
    gates = jax.nn.softmax(c_top, axis=-1).astype(h.dtype)
    blk = min(PEER_BLOCK, n)
    nb = -(-n // blk)
    pad = nb * blk - n

    def blocks(a):
        a = jnp.pad(a, ((0, pad),) + ((0, 0),) * (a.ndim - 1))
        return a.reshape((nb, blk) + a.shape[1:])

    def block(args):
        xb, ib, gb = args
        act = jax.nn.gelu(jnp.einsum('td,thkd->thk', xb, u_tab[ib]), approximate=False) * gb
        return jnp.einsum('thk,thkd->td', act, v_tab[ib])

    out = lax.map(block, (blocks(xf), blocks(ids), blocks(gates)))
    return out.reshape(nb * blk, D_MODEL)[:n].reshape(lead + (D_MODEL,))


def setup_inputs(seed: int = 0) -> dict:
    key = jax.random.key(seed)
    keys = list(jax.random.split(key, 48))

    def nrm(shape, scale):
        return jax.random.normal(keys.pop(), shape, jnp.float32) * scale

    def gain(shape):
        return 1.0 + nrm(shape, 0.02)

    n_pages = PAST_LEN // PAGE_SIZE
    n_used = DEC_BATCH * n_pages
    n_pool = n_used + max(1, n_used // 4)
    page_table = jax.random.permutation(keys.pop(), n_pool)[:n_used].reshape(
        DEC_BATCH, n_pages).astype(jnp.int32)
    return {
        'x_prompt': nrm((BATCH, SEQ, D_MODEL), 1.0),
        'x_sample': nrm((DEC_BATCH, DEC_SEQ, D_MODEL), 1.0),
        'cache_ckv': nrm((N_C, n_pool, PAGE_SIZE, R_KV), 1.0),
        'cache_kpe': nrm((N_C, n_pool, PAGE_SIZE, D_ROPE), 1.0),
        'state_conv': nrm((N_AB, DEC_BATCH, CONV_W - 1, D_A), 0.5),
        'state_gla': nrm((N_AB, DEC_BATCH, H_B, DK_B, DV_B), 0.5),
        'page_table': page_table,
        'ab_norm': gain((N_AB, D_MODEL)),
        'ab_w_in': nrm((N_AB, D_MODEL, D_IN_AB), D_MODEL ** -0.5),
        'conv_w': nrm((N_AB, CONV_W, D_A), CONV_W ** -0.5),
        'conv_b': nrm((N_AB, D_A), 0.02),
        'conv_ln_g': gain((N_AB, D_A)),
        'conv_ln_b': nrm((N_AB, D_A), 0.02),
        'gla_w_gate': nrm((N_AB, GATE_RANK, D_BK), GATE_RANK ** -0.5),
        'gla_b_gate': nrm((N_AB, D_BK), 0.1),
        'gla_head_g': gain((N_AB, DV_B)),
        'ab_w_out': nrm((N_AB, D_A + D_BV, D_MODEL), (D_A + D_BV) ** -0.5),
        'c_norm': gain((N_C, D_MODEL)),
        'c_w_in': nrm((N_C, D_MODEL, D_IN_C), D_MODEL ** -0.5),
        'c_q_norm': gain((N_C, R_Q)),
        'c_kv_norm': gain((N_C, R_KV)),
        'c_w_q_b': nrm((N_C, R_Q, H_C * (D_NOPE + D_ROPE)), R_Q ** -0.5),
        'c_w_uk': nrm((N_C, R_KV, H_C, D_NOPE), R_KV ** -0.5),
        'c_w_uv': nrm((N_C, R_KV, H_C, D_V), R_KV ** -0.5),
        'c_w_out': nrm((N_C, H_C * D_V, D_MODEL), (H_C * D_V) ** -0.5),
        'ffn_norm': gain((DEPTH, D_MODEL)),
        'peer_w_q': nrm((DEPTH, D_MODEL, H_P * D_KEY), D_MODEL ** -0.5),
        'peer_keys': nrm((DEPTH, H_P, 2, N_KEYS, D_HALF), D_HALF ** -0.5),
        'peer_u': nrm((DEPTH, N_EXPERTS, D_MODEL), D_MODEL ** -0.5),
        'peer_v': nrm((DEPTH, N_EXPERTS, D_MODEL), (H_P * TOPK) ** -0.5),
        'final_norm': gain((D_MODEL,)),
    }


def reference(x_prompt, x_sample, cache_ckv, cache_kpe, state_conv, state_gla, page_table,
              ab_norm, ab_w_in, conv_w, conv_b, conv_ln_g, conv_ln_b, gla_w_gate, gla_b_gate,
              gla_head_g, ab_w_out, c_norm, c_w_in, c_q_norm, c_kv_norm, c_w_q_b, c_w_uk,
              c_w_uv, c_w_out, ffn_norm, peer_w_q, peer_keys, peer_u, peer_v, final_norm):
    xp, xs = x_prompt, x_sample
    bp, tp = xp.shape[0], xp.shape[1]
    bs, ts = xs.shape[0], xs.shape[1]
    pos_p = jnp.arange(tp, dtype=jnp.int32)
    pos_s = PAST_LEN + jnp.arange(ts, dtype=jnp.int32)
    ckv_p, kpe_p, ckv_s, kpe_s = [], [], [], []
    conv_p, conv_s, gla_p, gla_s = [], [], [], []
    for l in range(DEPTH):
        j = l // 2
        if l % 2 == 0:
            prm = (ab_norm[j], ab_w_in[j], conv_w[j], conv_b[j], conv_ln_g[j], conv_ln_b[j],
                   gla_w_gate[j], gla_b_gate[j], gla_head_g[j], ab_w_out[j])
            zero_buf = jnp.zeros((bp, CONV_W - 1, D_A), xp.dtype)
            zero_s = jnp.zeros((bp, H_B, DK_B, DV_B), xp.dtype)
            xp, cb, sb = even_layer(xp, zero_buf, zero_s, *prm)
            conv_p.append(cb)
            gla_p.append(sb)
            xs, cb, sb = even_layer(xs, state_conv[j], state_gla[j], *prm)
            conv_s.append(cb)
            gla_s.append(sb)
        else:
            prm = (c_norm[j], c_w_in[j], c_q_norm[j], c_kv_norm[j], c_w_q_b[j],
                   c_w_uk[j], c_w_uv[j], c_w_out[j])
            xp, ck, kp = odd_layer(xp, pos_p, None, None, *prm)
            ckv_p.append(ck)
            kpe_p.append(kp)
            ckv_past = cache_ckv[j][page_table].reshape(bs, -1, R_KV)
            kpe_past = cache_kpe[j][page_table].reshape(bs, -1, D_ROPE)
            xs, ck, kp = odd_layer(xs, pos_s, ckv_past, kpe_past, *prm)
            ckv_s.append(ck)
            kpe_s.append(kp)
        xp = xp + peer(rms_norm(xp, ffn_norm[l]), peer_w_q[l], peer_keys[l], peer_u[l], peer_v[l])
        xs = xs + peer(rms_norm(xs, ffn_norm[l]), peer_w_q[l], peer_keys[l], peer_u[l], peer_v[l])
    y_prompt = rms_norm(xp, final_norm)
    y_sample = rms_norm(xs, final_norm)
    return (y_prompt, y_sample,
            jnp.stack(ckv_p), jnp.stack(kpe_p), jnp.stack(ckv_s), jnp.stack(kpe_s),
            jnp.stack(conv_p), jnp.stack(conv_s), jnp.stack(gla_p), jnp.stack(gla_s))
```

```python
import functools

import jax
import jax.numpy as jnp
from jax import lax
from jax.experimental import pallas as pl
from jax.experimental.pallas import tpu as pltpu

F32 = jnp.float32
BF16 = jnp.bfloat16
I32 = jnp.int32

D_MODEL = 1024
SEQ = 2048
DEPTH = 4
DEC_BATCH = 32
PAST_LEN = 16384
PAGE_SIZE = 128
EPS = 1e-6
D_A = 512
CONV_W = 31
H_B = 4
D_BV = 512
D_BK = 256
DK_B = 64
DV_B = 128
GATE_RANK = 16
GATE_TAU = 16.0
GLA_CHUNK = 64
GLA_SUB = 16
H_C = 16
D_NOPE = 64
D_ROPE = 32
D_V = 64
R_Q = 384
R_KV = 256
ROPE_BASE = 10000.0
MLA_SCALE = (D_NOPE + D_ROPE) ** -0.5
H_P = 8
N_KEYS = 128
N_EXPERTS = N_KEYS * N_KEYS
D_HALF = 128
TOPK = 16

LANES = 128
SAMPLE_ROWS = 128
KC_W = R_KV + LANES
ATT_TQ = 128
ATT_TK = 256
DEC_PAGES = 8
VMEM_LIMIT = 56 * 1024 * 1024
NEG_INF = float("-inf")


def _cparams(sem):
    return pltpu.CompilerParams(dimension_semantics=sem, vmem_limit_bytes=VMEM_LIMIT)


def _dot(a, b):
    return jnp.dot(a, b, preferred_element_type=F32)


def _dot_nt(a, b):
    return lax.dot_general(a, b, (((1,), (1,)), ((), ())), preferred_element_type=F32)


def _dot_tn(a, b):
    return lax.dot_general(a, b, (((0,), (0,)), ((), ())), preferred_element_type=F32)


def _rms(x, g):
    return x * lax.rsqrt(jnp.mean(x * x, axis=-1, keepdims=True) + EPS) * g


def _split3(x):
    hi = x.astype(BF16)
    r1 = x - hi.astype(F32)
    mid = r1.astype(BF16)
    lo = (r1 - mid.astype(F32)).astype(BF16)
    return hi, mid, lo


def _const_spec(shape):
    nd = len(shape)
    return pl.BlockSpec(shape, lambda *_: (0,) * nd)


def _norm_mm_kernel(x_ref, g_ref, w_ref, *out_refs, widths):
    xn = _rms(x_ref[...], g_ref[...])
    y = _dot(xn.astype(BF16), w_ref[...])
    off = 0
    for o_ref, w in zip(out_refs, widths):
        o_ref[...] = y[:, off:off + w]
        off += w


def norm_matmul(x, g, w, widths):
    n, k = x.shape
    tm = min(512, n)
    return pl.pallas_call(
        functools.partial(_norm_mm_kernel, widths=widths),
        grid=(n // tm,),
        in_specs=[pl.BlockSpec((tm, k), lambda i: (i, 0)), _const_spec((1, k)), _const_spec(w.shape)],
        out_specs=[pl.BlockSpec((tm, wd), lambda i: (i, 0)) for wd in widths],
        out_shape=[jax.ShapeDtypeStruct((n, wd), F32) for wd in widths],
        compiler_params=_cparams(("parallel",)),
        name="norm_matmul",
    )(x, g.reshape(1, k), w)


def _final_norm_kernel(x_ref, g_ref, o_ref):
    o_ref[...] = _rms(x_ref[...], g_ref[...])


def final_rms(x, g):
    n, k = x.shape
    tm = min(512, n)
    return pl.pallas_call(
        _final_norm_kernel,
        grid=(n // tm,),
        in_specs=[pl.BlockSpec((tm, k), lambda i: (i, 0)), _const_spec((1, k))],
        out_specs=pl.BlockSpec((tm, k), lambda i: (i, 0)),
        out_shape=jax.ShapeDtypeStruct((n, k), F32),
        compiler_params=_cparams(("parallel",)),
        name="final_norm",
    )(x, g.reshape(1, k))


def _ln_swish(c, g, b):
    mu = jnp.mean(c, axis=-1, keepdims=True)
    xc = c - mu
    y = xc * lax.rsqrt(jnp.mean(xc * xc, axis=-1, keepdims=True) + EPS) * g + b
    return y * jax.nn.sigmoid(y)


CONV_HIST = 32


def _conv_kernel(za_ref, cw_ref, cb_ref, lg_ref, lb_ref, out_ref, buf_ref, ext_ref, *, tt, nt):
    t = pl.program_id(1)

    @pl.when(t == 0)
    def _():
        ext_ref[0:CONV_HIST, :] = jnp.zeros((CONV_HIST, D_A), F32)

    za = za_ref[...]
    ext_ref[CONV_HIST:CONV_HIST + tt, :] = za[:, :D_A] * jax.nn.sigmoid(za[:, D_A:])
    first = CONV_HIST - (CONV_W - 1)
    acc = jnp.zeros((tt, D_A), F32) + cb_ref[...]
    for w in range(CONV_W):
        acc = acc + ext_ref[pl.ds(first + w, tt), :] * cw_ref[w:w + 1, :]
    out_ref[...] = _ln_swish(acc, lg_ref[...], lb_ref[...])

    @pl.when(t == nt - 1)
    def _():
        buf_ref[...] = ext_ref[pl.ds(tt + first, CONV_W - 1), :]

    ext_ref[0:CONV_HIST, :] = ext_ref[tt:tt + CONV_HIST, :]


def conv_branch_prompt(za, cw, cb, lg, lb, batch):
    n = za.shape[0]
    seq = n // batch
    tt = 512
    nt = seq // tt
    return pl.pallas_call(
        functools.partial(_conv_kernel, tt=tt, nt=nt),
        grid=(batch, nt),
        in_specs=[pl.BlockSpec((tt, 2 * D_A), lambda b, t: (b * nt + t, 0)),
                  _const_spec((CONV_W, D_A)), _const_spec((1, D_A)), _const_spec((1, D_A)),
                  _const_spec((1, D_A))],
        out_specs=[pl.BlockSpec((tt, D_A), lambda b, t: (b * nt + t, 0)),
                   pl.BlockSpec((None, CONV_W - 1, D_A), lambda b, t: (b, 0, 0))],
        out_shape=[jax.ShapeDtypeStruct((n, D_A), F32),
                   jax.ShapeDtypeStruct((batch, CONV_W - 1, D_A), F32)],
        scratch_shapes=[pltpu.VMEM((tt + CONV_HIST, D_A), F32)],
        compiler_params=_cparams(("parallel", "arbitrary")),
        name="conv_branch_prompt",
    )(za, cw, cb.reshape(1, D_A), lg.reshape(1, D_A), lb.reshape(1, D_A))


def _log_decay(zg, wg, bg):
    x = _dot(zg.astype(BF16), wg) + bg
    return (jnp.minimum(x, 0.0) - jnp.log1p(jnp.exp(-jnp.abs(x)))) * (1.0 / GATE_TAU)


def _head_norm_gate(o, hg, r):
    outs = []
    for h in range(H_B):
        oh = o[:, h * DV_B:(h + 1) * DV_B]
        outs.append(oh * lax.rsqrt(jnp.mean(oh * oh, axis=-1, keepdims=True) + EPS) * hg)
    return jnp.concatenate(outs, axis=1) * (r * jax.nn.sigmoid(r))


def _gla_kernel(zqkv_ref, zr_ref, zg_ref, wg_ref, bg_ref, hg_ref, ltri_ref, ones_ref,
                bout_ref, sfin_ref, s_ref, *, tt, nt):
    t = pl.program_id(1)

    @pl.when(t == 0)
    def _():
        s_ref[...] = jnp.zeros((D_BK, D_BV), F32)

    zq = zqkv_ref[...]
    q = zq[:, 0:D_BK] * (DK_B ** -0.5)
    k = zq[:, D_BK:2 * D_BK]
    v = zq[:, 2 * D_BK:]
    la = _log_decay(zg_ref[...], wg_ref[...], bg_ref[...])
    la3 = _split3(la)
    ltri = ltri_ref[...]
    b = _dot(ltri, la3[0]) + _dot(ltri, la3[1]) + _dot(ltri, la3[2])
    ones = ones_ref[...]

    lane_head = lax.broadcasted_iota(I32, (GLA_SUB, D_BK), 1) // DK_B
    bd_mask = (lax.broadcasted_iota(I32, (D_BK, D_BV), 0) // DK_B
               == lax.broadcasted_iota(I32, (D_BK, D_BV), 1) // DV_B)
    o_chunks = []
    for c in range(tt // GLA_CHUNK):
        r0 = c * GLA_CHUNK
        bc = b[r0:r0 + GLA_CHUNK]
        qc = q[r0:r0 + GLA_CHUNK]
        kc = k[r0:r0 + GLA_CHUNK]
        vc = v[r0:r0 + GLA_CHUNK]
        vcb = vc.astype(BF16)
        s_prev = s_ref[...]
        o = _dot((qc * jnp.exp(bc)).astype(BF16), s_prev.astype(BF16))
        o_rows = []
        for i in range(GLA_CHUNK // GLA_SUB):
            i0 = i * GLA_SUB
            nk = i0 + GLA_SUB
            ref_b = bc[i0:i0 + 1, :]
            qi = qc[i0:nk] * jnp.exp(bc[i0:nk] - ref_b)
            ki = kc[0:nk] * jnp.exp(ref_b - bc[0:nk])
            qs = jnp.concatenate([jnp.where(lane_head == h, qi, 0.0) for h in range(H_B)], axis=0)
            att = _dot_nt(qs.astype(BF16), ki.astype(BF16))
            row_i = lax.broadcasted_iota(I32, (H_B * GLA_SUB, nk), 0) % GLA_SUB
            col_j = lax.broadcasted_iota(I32, (H_B * GLA_SUB, nk), 1)
            att = jnp.where(col_j <= row_i + i0, att, 0.0)
            res = _dot(att.astype(BF16), vcb[0:nk])
            o_rows.append(jnp.concatenate(
                [res[h * GLA_SUB:(h + 1) * GLA_SUB, h * DV_B:(h + 1) * DV_B] for h in range(H_B)], axis=1))
        o_chunks.append(o + jnp.concatenate(o_rows, axis=0))
        b_end = bc[GLA_CHUNK - 1:GLA_CHUNK, :]
        kst = kc * jnp.exp(b_end - bc)
        kv = _dot_tn(kst.astype(BF16), vcb)
        b_end_t = (_dot_tn(la3[0][r0:r0 + GLA_CHUNK], ones) + _dot_tn(la3[1][r0:r0 + GLA_CHUNK], ones)
                   + _dot_tn(la3[2][r0:r0 + GLA_CHUNK], ones))
        dec = jnp.exp(b_end_t)
        s_ref[...] = s_prev * jnp.concatenate([dec] * H_B, axis=1) + jnp.where(bd_mask, kv, 0.0)

    o_all = jnp.concatenate(o_chunks, axis=0)
    bout_ref[...] = _head_norm_gate(o_all, hg_ref[...], zr_ref[...])

    @pl.when(t == nt - 1)
    def _():
        for h in range(H_B):
            sfin_ref[h] = s_ref[h * DK_B:(h + 1) * DK_B, h * DV_B:(h + 1) * DV_B]


def gla_prompt(zqkv, zr, zg, wg, bg, hg, batch):
    n = zqkv.shape[0]
    seq = n // batch
    tt = 256
    nt = seq // tt
    ri = lax.broadcasted_iota(I32, (tt, tt), 0)
    ci = lax.broadcasted_iota(I32, (tt, tt), 1)
    ltri = ((ri // GLA_CHUNK == ci // GLA_CHUNK) & (ci <= ri)).astype(BF16)
    ones = jnp.ones((GLA_CHUNK, LANES), BF16)
    row = lambda b, t: (b * nt + t, 0)
    return pl.pallas_call(
        functools.partial(_gla_kernel, tt=tt, nt=nt),
        grid=(batch, nt),
        in_specs=[pl.BlockSpec((tt, 2 * D_BK + D_BV), row), pl.BlockSpec((tt, D_BV), row),
                  pl.BlockSpec((tt, LANES), row), _const_spec((LANES, D_BK)), _const_spec((1, D_BK)),
                  _const_spec((1, DV_B)), _const_spec((tt, tt)), _const_spec((GLA_CHUNK, LANES))],
        out_specs=[pl.BlockSpec((tt, D_BV), row),
                   pl.BlockSpec((None, H_B, DK_B, DV_B), lambda b, t: (b, 0, 0, 0))],
        out_shape=[jax.ShapeDtypeStruct((n, D_BV), F32),
                   jax.ShapeDtypeStruct((batch, H_B, DK_B, DV_B), F32)],
        scratch_shapes=[pltpu.VMEM((D_BK, D_BV), F32)],
        compiler_params=_cparams(("parallel", "arbitrary")),
        name="gla_prompt",
    )(zqkv, zr, zg, wg, bg.reshape(1, D_BK), hg.reshape(1, DV_B), ltri, ones)


def _even_sample_a_kernel(za_ref, zqkv_ref, zg_ref, st_ref, cw_ref, cb_ref, lg_ref, lb_ref, wg_ref, bg_ref,
                          u_ref, aout_ref, q_ref, k_ref, a_ref):
    za = za_ref[...]
    u = za[:, :D_A] * jax.nn.sigmoid(za[:, D_A:])
    u_ref[...] = u
    acc = jnp.zeros_like(u) + cb_ref[...]
    for w in range(CONV_W - 1):
        acc = acc + st_ref[w] * cw_ref[w:w + 1, :]
    acc = acc + u * cw_ref[CONV_W - 1:CONV_W, :]
    aout_ref[...] = _ln_swish(acc, lg_ref[...], lb_ref[...])
    zq = zqkv_ref[...]
    q_ref[...] = zq[:, 0:D_BK] * (DK_B ** -0.5)
    k_ref[...] = zq[:, D_BK:2 * D_BK]
    a_ref[...] = jnp.exp(_log_decay(zg_ref[...], wg_ref[...], bg_ref[...]))


def even_sample_a(za, zqkv, zg, st_t, cw, cb, lg, lb, wg, bg):
    n = za.shape[0]
    shp = lambda w: jax.ShapeDtypeStruct((n, w), F32)
    return pl.pallas_call(
        _even_sample_a_kernel,
        out_shape=[shp(D_A), shp(D_A), shp(D_BK), shp(D_BK), shp(D_BK)],
        compiler_params=pltpu.CompilerParams(vmem_limit_bytes=VMEM_LIMIT),
        name="even_sample_a",
    )(za, zqkv, zg, st_t, cw, cb.reshape(1, D_A), lg.reshape(1, D_A), lb.reshape(1, D_A), wg,
      bg.reshape(1, D_BK))


def _even_sample_b_kernel(a_ref, k_ref, q_ref, v_ref, r_ref, s_ref, hg_ref, snew_ref, bout_ref):
    v = v_ref[...]
    outs = []
    for h in range(H_B):
        rows = slice(h * DK_B, (h + 1) * DK_B)
        vh = v[:, h * DV_B:(h + 1) * DV_B]
        s_new = a_ref[rows, :] * s_ref[h] + k_ref[rows, :] * vh
        snew_ref[h] = s_new
        outs.append(jnp.sum(q_ref[rows, :] * s_new, axis=0, keepdims=True))
    bout_ref[...] = _head_norm_gate(jnp.concatenate(outs, axis=1), hg_ref[...], r_ref[...])


def even_sample_b(a3, k3, q3, v3, r3, state, hg):
    nb = a3.shape[0]
    col = pl.BlockSpec((None, D_BK, 1), lambda b: (b, 0, 0))
    rowv = pl.BlockSpec((None, 1, D_BV), lambda b: (b, 0, 0))
    st = pl.BlockSpec((None, H_B, DK_B, DV_B), lambda b: (b, 0, 0, 0))
    return pl.pallas_call(
        _even_sample_b_kernel,
        grid=(nb,),
        in_specs=[col, col, col, rowv, rowv, st, _const_spec((1, DV_B))],
        out_specs=[st, rowv],
        out_shape=[jax.ShapeDtypeStruct((nb, H_B, DK_B, DV_B), F32),
                   jax.ShapeDtypeStruct((nb, 1, D_BV), F32)],
        compiler_params=_cparams(("parallel",)),
        name="even_sample_b",
    )(a3, k3, q3, v3, r3, state, hg.reshape(1, DV_B))


def _even_out_kernel(x_ref, a_ref, b_ref, w_ref, o_ref):
    y = _dot(a_ref[...].astype(BF16), w_ref[0:D_A, :]) + _dot(b_ref[...].astype(BF16), w_ref[D_A:, :])
    o_ref[...] = x_ref[...] + y


def even_out(x, a, b, w):
    n = x.shape[0]
    tm = min(512, n)
    row = lambda i: (i, 0)
    return pl.pallas_call(
        _even_out_kernel,
        grid=(n // tm,),
        in_specs=[pl.BlockSpec((tm, D_MODEL), row), pl.BlockSpec((tm, D_A), row),
                  pl.BlockSpec((tm, D_BV), row), _const_spec(w.shape)],
        out_specs=pl.BlockSpec((tm, D_MODEL), row),
        out_shape=jax.ShapeDtypeStruct((n, D_MODEL), F32),
        compiler_params=_cparams(("parallel",)),
        name="even_out",
    )(x, a, b, w)


def _mla_prep_kernel(cq_ref, ckv_ref, kpe_ref, kper_ref, cs_ref, sn_ref, qg_ref, kvg_ref,
                     wn_ref, wa_ref, wb_ref, wuk_ref, qc_ref, kc_ref, ckvo_ref, kpeo_ref):
    cs = cs_ref[...]
    sn = sn_ref[...]
    ckv = _rms(ckv_ref[...], kvg_ref[...])
    kpe = kpe_ref[...] * cs + kper_ref[...] * sn
    ckvo_ref[...] = ckv
    kpeo_ref[...] = kpe
    kc_ref[:, 0:R_KV] = ckv.astype(BF16)
    kc_ref[:, R_KV:] = kpe.astype(BF16)
    cqn = _rms(cq_ref[...], qg_ref[...]).astype(BF16)
    qn = _dot(cqn, wn_ref[...]).astype(BF16)
    qa = _dot(cqn, wa_ref[...])
    qb = _dot(cqn, wb_ref[...])
    for h in range(H_C):
        pair = qn[:, (h // 2) * LANES:(h // 2 + 1) * LANES]
        ql = _dot(pair, wuk_ref[h]) * MLA_SCALE
        qp = (qa[:, h * LANES:(h + 1) * LANES] * cs + qb[:, h * LANES:(h + 1) * LANES] * sn) * MLA_SCALE
        qc_ref[h, :, 0:R_KV] = ql.astype(BF16)
        qc_ref[h, :, R_KV:] = qp.astype(BF16)


def mla_prep(cq, ckv, kpe, kper, cs, sn, qg, kvg, wn, wa, wb, wuk, pos_tiles):
    n = cq.shape[0]
    tm = ATT_TQ
    row = lambda i: (i, 0)
    pos = lambda i: (i % pos_tiles, 0)
    return pl.pallas_call(
        _mla_prep_kernel,
        grid=(n // tm,),
        in_specs=[pl.BlockSpec((tm, R_Q), row), pl.BlockSpec((tm, R_KV), row),
                  pl.BlockSpec((tm, LANES), row), pl.BlockSpec((tm, LANES), row),
                  pl.BlockSpec((tm, LANES), pos), pl.BlockSpec((tm, LANES), pos),
                  _const_spec((1, R_Q)), _const_spec((1, R_KV)),
                  _const_spec(wn.shape), _const_spec(wa.shape), _const_spec(wb.shape), _const_spec(wuk.shape)],
        out_specs=[pl.BlockSpec((None, H_C, tm, KC_W), lambda i: (i, 0, 0, 0)),
                   pl.BlockSpec((tm, KC_W), row), pl.BlockSpec((tm, R_KV), row),
                   pl.BlockSpec((tm, LANES), row)],
        out_shape=[jax.ShapeDtypeStruct((n // tm, H_C, tm, KC_W), BF16),
                   jax.ShapeDtypeStruct((n, KC_W), BF16),
                   jax.ShapeDtypeStruct((n, R_KV), F32),
                   jax.ShapeDtypeStruct((n, LANES), F32)],
        compiler_params=_cparams(("parallel",)),
        name="mla_prep",
    )(cq, ckv, kpe, kper, cs, sn, qg.reshape(1, R_Q), kvg.reshape(1, R_KV), wn, wa, wb, wuk)


def _mla_attn_kernel(qc_ref, kc_ref, ol_ref, m_ref, l_ref, acc_ref):
    i = pl.program_id(1)
    j = pl.program_id(2)
    rows = H_C * ATT_TQ
    last = (i * ATT_TQ) // ATT_TK

    @pl.when(j == 0)
    def _():
        m_ref[...] = jnp.full((rows, 1), NEG_INF, F32)
        l_ref[...] = jnp.zeros((rows, 1), F32)
        acc_ref[...] = jnp.zeros((rows, R_KV), F32)

    @pl.when(j <= last)
    def _():
        q = qc_ref[...].reshape(rows, KC_W)
        kc = kc_ref[...]
        s = _dot_nt(q, kc)
        tok = i * ATT_TQ + lax.broadcasted_iota(I32, (rows, ATT_TK), 0) % ATT_TQ
        key = j * ATT_TK + lax.broadcasted_iota(I32, (rows, ATT_TK), 1)
        s = jnp.where(key <= tok, s, NEG_INF)
        m_old = m_ref[...]
        m_new = jnp.maximum(m_old, jnp.max(s, axis=1, keepdims=True))
        p = jnp.exp(s - m_new)
        alpha = jnp.exp(m_old - m_new)
        l_ref[...] = alpha * l_ref[...] + jnp.sum(p, axis=1, keepdims=True)
        acc_ref[...] = alpha * acc_ref[...] + _dot(p.astype(BF16), kc[:, 0:R_KV])
        m_ref[...] = m_new

    @pl.when(j == last)
    def _():
        ol_ref[...] = (acc_ref[...] / l_ref[...]).reshape(H_C, ATT_TQ, R_KV).astype(BF16)


def mla_attn_prompt(qc, kc, batch):
    ntile = qc.shape[0]
    nq = ntile // batch
    nk = (nq * ATT_TQ) // ATT_TK
    rows = H_C * ATT_TQ
    kv_idx = lambda b, i, j: (b * nk + jnp.minimum(j, (i * ATT_TQ) // ATT_TK), 0)
    return pl.pallas_call(
        _mla_attn_kernel,
        grid=(batch, nq, nk),
        in_specs=[pl.BlockSpec((None, H_C, ATT_TQ, KC_W), lambda b, i, j: (b * nq + i, 0, 0, 0)),
                  pl.BlockSpec((ATT_TK, KC_W), kv_idx)],
        out_specs=pl.BlockSpec((None, H_C, ATT_TQ, R_KV), lambda b, i, j: (b * nq + i, 0, 0, 0)),
        out_shape=jax.ShapeDtypeStruct((ntile, H_C, ATT_TQ, R_KV), BF16),
        scratch_shapes=[pltpu.VMEM((rows, 1), F32), pltpu.VMEM((rows, 1), F32),
                        pltpu.VMEM((rows, R_KV), F32)],
        compiler_params=_cparams(("parallel", "parallel", "arbitrary")),
        name="mla_attn_prompt",
    )(qc, kc)


def _mla_decode_kernel(pt_ref, q_ref, kcur_ref, *refs, ng):
    ckv_refs = refs[:DEC_PAGES]
    kpe_refs = refs[DEC_PAGES:2 * DEC_PAGES]
    o_ref, m_ref, l_ref, acc_ref = refs[2 * DEC_PAGES:]
    g = pl.program_id(1)

    @pl.when(g == 0)
    def _():
        m_ref[...] = jnp.full((H_C, 1), NEG_INF, F32)
        l_ref[...] = jnp.zeros((H_C, 1), F32)
        acc_ref[...] = jnp.zeros((H_C, R_KV), F32)

    q = q_ref[...]
    ql = q[:, 0:R_KV]
    qp = q[:, R_KV:R_KV + D_ROPE]
    cks = [r[...].astype(BF16) for r in ckv_refs]
    s = jnp.concatenate(
        [_dot_nt(ql, cks[p]) + _dot_nt(qp, kpe_refs[p][...].astype(BF16)) for p in range(DEC_PAGES)], axis=1)
    m_old = m_ref[...]
    m_new = jnp.maximum(m_old, jnp.max(s, axis=1, keepdims=True))
    p_all = jnp.exp(s - m_new).astype(BF16)
    alpha = jnp.exp(m_old - m_new)
    pv = _dot(p_all[:, 0:PAGE_SIZE], cks[0])
    for p in range(1, DEC_PAGES):
        pv = pv + _dot(p_all[:, p * PAGE_SIZE:(p + 1) * PAGE_SIZE], cks[p])
    l_new = alpha * l_ref[...] + jnp.sum(p_all.astype(F32), axis=1, keepdims=True)
    acc_new = alpha * acc_ref[...] + pv
    m_ref[...] = m_new
    l_ref[...] = l_new
    acc_ref[...] = acc_new

    @pl.when(g == ng - 1)
    def _():
        kcur = kcur_ref[...].astype(F32)
        s_cur = jnp.sum(q.astype(F32) * kcur, axis=1, keepdims=True)
        m_fin = jnp.maximum(m_new, s_cur)
        a_fin = jnp.exp(m_new - m_fin)
        p_cur = jnp.exp(s_cur - m_fin)
        l_fin = a_fin * l_new + p_cur
        o_ref[...] = (a_fin * acc_new + p_cur * kcur[:, 0:R_KV]) / l_fin


def mla_decode(page_table, q_s, kc_s, cache_ckv, cache_kpe, layer):
    nb, npages = page_table.shape
    ng = npages // DEC_PAGES

    def page_spec(width, p):
        return pl.BlockSpec((None, None, PAGE_SIZE, width),
                            lambda b, g, pt: (layer, pt[b, g * DEC_PAGES + p], 0, 0))

    grid_spec = pltpu.PrefetchScalarGridSpec(
        num_scalar_prefetch=1,
        grid=(nb, ng),
        in_specs=([pl.BlockSpec((None, H_C, KC_W), lambda b, g, pt: (b, 0, 0)),
                   pl.BlockSpec((None, 1, KC_W), lambda b, g, pt: (b, 0, 0))]
                  + [page_spec(R_KV, p) for p in range(DEC_PAGES)]
                  + [page_spec(D_ROPE, p) for p in range(DEC_PAGES)]),
        out_specs=pl.BlockSpec((None, H_C, R_KV), lambda b, g, pt: (b, 0, 0)),
        scratch_shapes=[pltpu.VMEM((H_C, 1), F32), pltpu.VMEM((H_C, 1), F32), pltpu.VMEM((H_C, R_KV), F32)],
    )
    return pl.pallas_call(
        functools.partial(_mla_decode_kernel, ng=ng),
        grid_spec=grid_spec,
        out_shape=jax.ShapeDtypeStruct((nb, H_C, R_KV), F32),
        compiler_params=_cparams(("parallel", "arbitrary")),
        name="mla_decode",
    )(page_table, q_s, kc_s, *([cache_ckv] * DEC_PAGES), *([cache_kpe] * DEC_PAGES))


def _mla_out_kernel(x_ref, ol_ref, wuv_ref, wo_ref, o_ref):
    pairs = []
    for p in range(H_C // 2):
        pairs.append((_dot(ol_ref[2 * p], wuv_ref[2 * p]) + _dot(ol_ref[2 * p + 1], wuv_ref[2 * p + 1])).astype(BF16))
    o = jnp.concatenate(pairs, axis=1)
    o_ref[...] = x_ref[...] + _dot(o, wo_ref[...])


def mla_out(x, ol, wuv, wo):
    n = x.shape[0]
    tm = ATT_TQ
    return pl.pallas_call(
        _mla_out_kernel,
        grid=(n // tm,),
        in_specs=[pl.BlockSpec((tm, D_MODEL), lambda i: (i, 0)),
                  pl.BlockSpec((None, H_C, tm, R_KV), lambda i: (i, 0, 0, 0)),
                  _const_spec(wuv.shape), _const_spec(wo.shape)],
        out_specs=pl.BlockSpec((tm, D_MODEL), lambda i: (i, 0)),
        out_shape=jax.ShapeDtypeStruct((n, D_MODEL), F32),
        compiler_params=_cparams(("parallel",)),
        name="mla_out",
    )(x, ol, wuv, wo)


def _top16_rows(s):
    rows, tm = s.shape
    iota = lax.broadcasted_iota(I32, (rows, tm), 0)
    rank = jnp.full((rows, tm), TOPK, I32)
    vals = []
    cur = s
    for it in range(TOPK):
        m = jnp.max(cur, axis=0, keepdims=True)
        idx = jnp.min(jnp.where(cur == m, iota, rows), axis=0, keepdims=True)
        sel = iota == idx
        rank = jnp.where(sel, it, rank)
        cur = jnp.where(sel, NEG_INF, cur)
        vals.append(m)
    return jnp.concatenate(vals, axis=0), rank


CAND_ROWS = 80


def _pair_top16(v1, v2):
    tm = v1.shape[1]
    pieces = [v1[0:1] + v2]
    for a in range(1, 8):
        pieces.append(v1[a:a + 1] + v2[0:8])
    pieces.append(v1[8:16] + v2[0:1])
    cand = jnp.concatenate(pieces, axis=0)
    r = lax.broadcasted_iota(I32, (CAND_ROWS, tm), 0)
    a_mid = 1 + ((r - 16) >> 3)
    b_mid = (r - 16) & 7
    a_idx = jnp.where(r < 16, 0, jnp.where(r < 72, a_mid, r - 64))
    b_idx = jnp.where(r < 16, r, jnp.where(r < 72, b_mid, 0))
    flat = a_idx * TOPK + b_idx
    cand = jnp.where((a_idx + 1) * (b_idx + 1) <= TOPK, cand, NEG_INF)
    sums, a_sel = [], []
    for _ in range(TOPK):
        m = jnp.max(cand, axis=0, keepdims=True)
        ci = jnp.min(jnp.where(cand == m, flat, TOPK * TOPK), axis=0, keepdims=True)
        cand = jnp.where(flat == ci, NEG_INF, cand)
        sums.append(m)
        a_sel.append(ci >> 4)
    return sums, a_sel


def _peer_route_kernel(x_ref, g_ref, wq_ref, keys_ref, xn_ref, r2_ref, bt_ref, e1_ref, e2_ref):
    xn = _rms(x_ref[...], g_ref[...]).astype(BF16)
    xn_ref[...] = xn
    q = _dot(xn, wq_ref[...]).astype(BF16)
    tm = q.shape[0]
    iota16 = lax.broadcasted_iota(I32, (TOPK, tm), 0)
    for h in range(H_P):
        s1 = _dot_nt(keys_ref[h, 0], q[:, (2 * h) * D_HALF:(2 * h + 1) * D_HALF])
        s2 = _dot_nt(keys_ref[h, 1], q[:, (2 * h + 1) * D_HALF:(2 * h + 2) * D_HALF])
        v1, rank1 = _top16_rows(s1)
        v2, rank2 = _top16_rows(s2)
        sums, a_sel = _pair_top16(v1, v2)
        z = jnp.zeros((1, tm), F32)
        cnt = jnp.zeros((TOPK, tm), F32)
        for kk in range(TOPK):
            z = z + jnp.exp(sums[kk] - sums[0])
            cnt = cnt + jnp.where(iota16 == a_sel[kk], 1.0, 0.0)
        bt = jnp.zeros((N_KEYS, tm), F32)
        for a in range(TOPK):
            bt = bt + jnp.where(rank1 == a, cnt[a:a + 1], 0.0)
        bt_ref[h] = bt
        e1_ref[h] = jnp.exp(s1 - v1[0:1]) / z
        e2_ref[h] = jnp.exp(s2 - v2[0:1])
        r2_ref[h] = rank2.astype(F32)


def peer_route(x, g, wq, keys):
    n = x.shape[0]
    tm = min(256, n)
    tab = pl.BlockSpec((H_P, N_KEYS, tm), lambda i: (0, 0, i))
    tab_shape = jax.ShapeDtypeStruct((H_P, N_KEYS, n), F32)
    return pl.pallas_call(
        _peer_route_kernel,
        grid=(n // tm,),
        in_specs=[pl.BlockSpec((tm, D_MODEL), lambda i: (i, 0)), _const_spec((1, D_MODEL)),
                  _const_spec(wq.shape), _const_spec(keys.shape)],
        out_specs=[pl.BlockSpec((tm, D_MODEL), lambda i: (i, 0)), tab, tab, tab, tab],
        out_shape=[jax.ShapeDtypeStruct((n, D_MODEL), BF16), tab_shape, tab_shape, tab_shape, tab_shape],
        compiler_params=_cparams(("parallel",)),
        name="peer_route",
    )(x, g.reshape(1, D_MODEL), wq, keys)


def _peer_dense_kernel(x_ref, xn_ref, r2_ref, bt_ref, e1_ref, e2_ref, u_ref, vt_ref, o_ref,
                       acc_ref, act_ref, *, te, ne):
    e = pl.program_id(1)

    @pl.when(e == 0)
    def _():
        acc_ref[...] = jnp.zeros_like(acc_ref)

    xn = xn_ref[...]
    per_step = te // N_KEYS
    for ii in range(per_step):
        i1 = e * per_step + ii
        ht = _dot_nt(u_ref[ii * N_KEYS:(ii + 1) * N_KEYS, :], xn)
        gate = jnp.zeros_like(ht)
        for h in range(H_P):
            sel = r2_ref[h] < bt_ref[h, pl.ds(i1, 1), :]
            gate = gate + jnp.where(sel, e2_ref[h] * e1_ref[h, pl.ds(i1, 1), :], 0.0)
        act = 0.5 * ht * (1.0 + lax.erf(ht * (2.0 ** -0.5))) * gate
        act_ref[ii * N_KEYS:(ii + 1) * N_KEYS, :] = act.astype(BF16)
    acc_ref[...] += _dot(vt_ref[...], act_ref[...])

    @pl.when(e == ne - 1)
    def _():
        o_ref[...] = x_ref[...] + acc_ref[...].T


def peer_dense(x, xn, r2, bt, e1, e2, u, vt):
    n = x.shape[0]
    tm = min(512, n)
    te = 512
    ne = N_EXPERTS // te
    tab = pl.BlockSpec((H_P, N_KEYS, tm), lambda i, e: (0, 0, i))
    return pl.pallas_call(
        functools.partial(_peer_dense_kernel, te=te, ne=ne),
        grid=(n // tm, ne),
        in_specs=[pl.BlockSpec((tm, D_MODEL), lambda i, e: (i, 0)),
                  pl.BlockSpec((tm, D_MODEL), lambda i, e: (i, 0)),
                  tab, tab, tab, tab,
                  pl.BlockSpec((te, D_MODEL), lambda i, e: (e, 0)),
                  pl.BlockSpec((D_MODEL, te), lambda i, e: (0, e))],
        out_specs=pl.BlockSpec((tm, D_MODEL), lambda i, e: (i, 0)),
        out_shape=jax.ShapeDtypeStruct((n, D_MODEL), F32),
        scratch_shapes=[pltpu.VMEM((D_MODEL, tm), F32), pltpu.VMEM((te, tm), BF16)],
        compiler_params=_cparams(("parallel", "arbitrary")),
        name="peer_dense",
    )(x, xn, r2, bt, e1, e2, u, vt)


def peer(x, g, wq, keys, u, vt):
    xn, r2, bt, e1, e2 = peer_route(x, g, wq, keys)
    return peer_dense(x, xn, r2, bt, e1, e2, u, vt)


def _pad_cols(w, width):
    return jnp.pad(w, ((0, 0), (0, width - w.shape[1])))


def _rope_tables(pos):
    half = D_ROPE // 2
    inv = jnp.power(ROPE_BASE, -jnp.arange(half, dtype=F32) / half)
    ang = pos.astype(F32)[:, None] * inv[None, :]
    cos2 = jnp.concatenate([jnp.cos(ang), jnp.cos(ang)], axis=1)
    sin2 = jnp.concatenate([jnp.sin(ang), jnp.sin(ang)], axis=1)
    return _pad_cols(cos2, LANES), _pad_cols(sin2, LANES)


def _rot_half_cols(w):
    half = D_ROPE // 2
    return jnp.concatenate([-w[..., half:], w[..., :half]], axis=-1)


def _even_layer_weights(w_in, w_gate, w_out):
    w_in_p = _pad_cols(w_in, 2 * D_A + 2 * D_BK + 2 * D_BV + LANES).astype(BF16)
    wg = jnp.pad(w_gate, ((0, LANES - GATE_RANK), (0, 0))).astype(BF16)
    return w_in_p, wg, w_out.astype(BF16)


def _odd_layer_weights(w_in, w_q_b, w_uk, w_uv, w_out):
    wk = w_in[:, R_Q + R_KV:]
    w_in_p = jnp.concatenate([w_in[:, :R_Q + R_KV], _pad_cols(wk, LANES), _pad_cols(_rot_half_cols(wk), LANES)],
                             axis=1).astype(BF16)
    wq3 = w_q_b.reshape(R_Q, H_C, D_NOPE + D_ROPE)
    wn = wq3[:, :, :D_NOPE].reshape(R_Q, H_C * D_NOPE).astype(BF16)
    wpe = wq3[:, :, D_NOPE:]
    pad3 = lambda w: jnp.pad(w, ((0, 0), (0, 0), (0, LANES - D_ROPE))).reshape(R_Q, H_C * LANES).astype(BF16)
    wa, wb = pad3(wpe), pad3(_rot_half_cols(wpe))
    ukt = jnp.transpose(w_uk, (1, 2, 0))
    uv = jnp.transpose(w_uv, (1, 0, 2))
    odd = (jnp.arange(H_C) % 2 == 1)[:, None, None]
    zk = jnp.zeros_like(ukt)
    wuk = jnp.where(odd, jnp.concatenate([zk, ukt], axis=1), jnp.concatenate([ukt, zk], axis=1)).astype(BF16)
    zv = jnp.zeros_like(uv)
    wuv = jnp.where(odd, jnp.concatenate([zv, uv], axis=2), jnp.concatenate([uv, zv], axis=2)).astype(BF16)
    return w_in_p, wn, wa, wb, wuk, wuv, w_out.astype(BF16)


def kernel(x_prompt, x_sample, cache_ckv, cache_kpe, state_conv, state_gla, page_table, ab_norm, ab_w_in,
           conv_w, conv_b, conv_ln_g, conv_ln_b, gla_w_gate, gla_b_gate, gla_head_g, ab_w_out, c_norm, c_w_in,
           c_q_norm, c_kv_norm, c_w_q_b, c_w_uk, c_w_uv, c_w_out, ffn_norm, peer_w_q, peer_keys, peer_u,
           peer_v, final_norm):
    bp, tp, _ = x_prompt.shape
    bs = x_sample.shape[0]
    n_p = bp * tp
    xp = x_prompt.reshape(n_p, D_MODEL)
    xs = jnp.pad(x_sample.reshape(bs, D_MODEL), ((0, SAMPLE_ROWS - bs), (0, 0)))
    cs_p, sn_p = _rope_tables(jnp.arange(tp, dtype=I32))
    cs_s, sn_s = _rope_tables(jnp.full((SAMPLE_ROWS,), PAST_LEN, I32))
    ckv_p, kpe_p, ckv_s, kpe_s = [], [], [], []
    conv_p, conv_s, gla_p, gla_s = [], [], [], []
    for l in range(DEPTH):
        j = l // 2
        if l % 2 == 0:
            w_in, wg, w_out = _even_layer_weights(ab_w_in[j], gla_w_gate[j], ab_w_out[j])
            widths = (2 * D_A, 2 * D_BK + D_BV, D_BV, LANES)
            za, zqkv, zr, zg = norm_matmul(xp, ab_norm[j], w_in, widths)
            a_out, cbuf = conv_branch_prompt(za, conv_w[j], conv_b[j], conv_ln_g[j], conv_ln_b[j], bp)
            b_out, s_fin = gla_prompt(zqkv, zr, zg, wg, gla_b_gate[j], gla_head_g[j], bp)
            xp = even_out(xp, a_out, b_out, w_out)
            conv_p.append(cbuf)
            gla_p.append(s_fin)
            za, zqkv, zr, zg = norm_matmul(xs, ab_norm[j], w_in, widths)
            st_t = jnp.pad(jnp.transpose(state_conv[j], (1, 0, 2)), ((0, 0), (0, SAMPLE_ROWS - bs), (0, 0)))
            u, a_out, qg, kg, ag = even_sample_a(za, zqkv, zg, st_t, conv_w[j], conv_b[j], conv_ln_g[j],
                                                 conv_ln_b[j], wg, gla_b_gate[j])
            col = lambda t: t[:bs].reshape(bs, D_BK, 1)
            s_new, b_out = even_sample_b(col(ag), col(kg), col(qg),
                                         zqkv[:bs, 2 * D_BK:].reshape(bs, 1, D_BV),
                                         zr[:bs].reshape(bs, 1, D_BV), state_gla[j], gla_head_g[j])
            b_out = jnp.pad(b_out.reshape(bs, D_BV), ((0, SAMPLE_ROWS - bs), (0, 0)))
            xs = even_out(xs, a_out, b_out, w_out)
            conv_s.append(jnp.concatenate([state_conv[j][:, 1:], u[:bs, None, :]], axis=1))
            gla_s.append(s_new)
        else:
            w_in, wn, wa, wb, wuk, wuv, w_out = _odd_layer_weights(c_w_in[j], c_w_q_b[j], c_w_uk[j], c_w_uv[j],
                                                                   c_w_out[j])
            widths = (R_Q, R_KV, LANES, LANES)
            cq, ckv, kpe, kper = norm_matmul(xp, c_norm[j], w_in, widths)
            qc, kc, ckv_o, kpe_o = mla_prep(cq, ckv, kpe, kper, cs_p, sn_p, c_q_norm[j], c_kv_norm[j],
                                            wn, wa, wb, wuk, tp // ATT_TQ)
            ol = mla_attn_prompt(qc, kc, bp)
            xp = mla_out(xp, ol, wuv, w_out)
            ckv_p.append(ckv_o.reshape(bp, tp, R_KV))
            kpe_p.append(kpe_o[:, :D_ROPE].reshape(bp, tp, D_ROPE))
            cq, ckv, kpe, kper = norm_matmul(xs, c_norm[j], w_in, widths)
            qc, kc, ckv_o, kpe_o = mla_prep(cq, ckv, kpe, kper, cs_s, sn_s, c_q_norm[j], c_kv_norm[j],
                                            wn, wa, wb, wuk, 1)
            q_s = jnp.transpose(qc[0, :, :bs, :], (1, 0, 2))
            o_lat = mla_decode(page_table, q_s, kc[:bs].reshape(bs, 1, KC_W), cache_ckv, cache_kpe, j)
            ol = jnp.pad(jnp.transpose(o_lat, (1, 0, 2)), ((0, 0), (0, SAMPLE_ROWS - bs), (0, 0)))
            xs = mla_out(xs, ol.astype(BF16)[None], wuv, w_out)
            ckv_s.append(ckv_o[:bs].reshape(bs, 1, R_KV))
            kpe_s.append(kpe_o[:bs, :D_ROPE].reshape(bs, 1, D_ROPE))
        wq = peer_w_q[l].astype(BF16)
        keys = peer_keys[l].astype(BF16)
        u_tab = peer_u[l].astype(BF16)
        vt_tab = peer_v[l].astype(BF16).T
        xp = peer(xp, ffn_norm[l], wq, keys, u_tab, vt_tab)
        xs = peer(xs, ffn_norm[l], wq, keys, u_tab, vt_tab)
    y_prompt = final_rms(xp, final_norm).reshape(bp, tp, D_MODEL)
    y_sample = final_rms(xs, final_norm)[:bs].reshape(bs, 1, D_MODEL)
    return (y_prompt, y_sample,
            jnp.stack(ckv_p), jnp.stack(kpe_p), jnp.stack(ckv_s), jnp.stack(kpe_s),
            jnp.stack(conv_p), jnp.stack(conv_s), jnp.stack(gla_p), jnp.stack(gla_s))
```

```python
import functools

import jax
import jax.numpy as jnp
from jax import lax
from jax.experimental import pallas as pl
from jax.experimental.pallas import tpu as pltpu

F32 = jnp.float32
BF16 = jnp.bfloat16
I32 = jnp.int32

D_MODEL = 1024
SEQ = 2048
DEPTH = 4
DEC_BATCH = 32
PAST_LEN = 16384
PAGE_SIZE = 128
EPS = 1e-6
D_A = 512
CONV_W = 31
H_B = 4
D_BV = 512
D_BK = 256
DK_B = 64
DV_B = 128
GATE_RANK = 16
GATE_TAU = 16.0
GLA_CHUNK = 64
GLA_SUB = 16
H_C = 16
D_NOPE = 64
D_ROPE = 32
D_V = 64
R_Q = 384
R_KV = 256
ROPE_BASE = 10000.0
MLA_SCALE = (D_NOPE + D_ROPE) ** -0.5
H_P = 8
N_KEYS = 128
N_EXPERTS = N_KEYS * N_KEYS
D_HALF = 128
TOPK = 16

LANES = 128
SUBLANES = 8
BF16_ROWS = 16
SAMPLE_ROWS = 128
KC_W = R_KV + LANES
ATT_TQ = 128
ATT_TK = 512
DEC_PAGES = 8
VMEM_LIMIT = 56 * 1024 * 1024
NEG_INF = float("-inf")


def _cparams(sem):
    return pltpu.CompilerParams(dimension_semantics=sem, vmem_limit_bytes=VMEM_LIMIT)


def _dot(a, b):
    return jnp.dot(a, b, preferred_element_type=F32)


def _dot_nt(a, b):
    return lax.dot_general(a, b, (((1,), (1,)), ((), ())), preferred_element_type=F32)


def _dot_tn(a, b):
    return lax.dot_general(a, b, (((0,), (0,)), ((), ())), preferred_element_type=F32)


def _rms(x, g):
    return x * lax.rsqrt(jnp.mean(x * x, axis=-1, keepdims=True) + EPS) * g


def _split3(x):
    hi = x.astype(BF16)
    r1 = x - hi.astype(F32)
    mid = r1.astype(BF16)
    lo = (r1 - mid.astype(F32)).astype(BF16)
    return hi, mid, lo


def _const_spec(shape):
    nd = len(shape)
    return pl.BlockSpec(shape, lambda *_: (0,) * nd)


def _norm_mm_kernel(x_ref, g_ref, w_ref, *out_refs, widths):
    xn = _rms(x_ref[...], g_ref[...])
    y = _dot(xn.astype(BF16), w_ref[...])
    off = 0
    for o_ref, w in zip(out_refs, widths):
        o_ref[...] = y[:, off:off + w]
        off += w


def norm_matmul(x, g, w, widths):
    n, k = x.shape
    tm = min(512, n)
    return pl.pallas_call(
        functools.partial(_norm_mm_kernel, widths=widths),
        grid=(n // tm,),
        in_specs=[pl.BlockSpec((tm, k), lambda i: (i, 0)), _const_spec((1, k)), _const_spec(w.shape)],
        out_specs=[pl.BlockSpec((tm, wd), lambda i: (i, 0)) for wd in widths],
        out_shape=[jax.ShapeDtypeStruct((n, wd), F32) for wd in widths],
        compiler_params=_cparams(("parallel",)),
        name="norm_matmul",
    )(x, g.reshape(1, k), w)


def _final_norm_kernel(x_ref, g_ref, o_ref):
    o_ref[...] = _rms(x_ref[...], g_ref[...])


def final_rms(x, g):
    n, k = x.shape
    tm = min(512, n)
    return pl.pallas_call(
        _final_norm_kernel,
        grid=(n // tm,),
        in_specs=[pl.BlockSpec((tm, k), lambda i: (i, 0)), _const_spec((1, k))],
        out_specs=pl.BlockSpec((tm, k), lambda i: (i, 0)),
        out_shape=jax.ShapeDtypeStruct((n, k), F32),
        compiler_params=_cparams(("parallel",)),
        name="final_norm",
    )(x, g.reshape(1, k))


def _ln_swish(c, g, b):
    mu = jnp.mean(c, axis=-1, keepdims=True)
    xc = c - mu
    y = xc * lax.rsqrt(jnp.mean(xc * xc, axis=-1, keepdims=True) + EPS) * g + b
    return y * jax.nn.sigmoid(y)


CONV_HIST = 32


def _conv_kernel(za_ref, cw_ref, cb_ref, lg_ref, lb_ref, out_ref, buf_ref, ext_ref, *, tt, nt):
    t = pl.program_id(1)

    @pl.when(t == 0)
    def _():
        ext_ref[0:CONV_HIST, :] = jnp.zeros((CONV_HIST, D_A), F32)

    za = za_ref[...]
    ext_ref[CONV_HIST:CONV_HIST + tt, :] = za[:, :D_A] * jax.nn.sigmoid(za[:, D_A:])
    first = CONV_HIST - (CONV_W - 1)
    acc = jnp.zeros((tt, D_A), F32) + cb_ref[...]
    for w in range(CONV_W):
        acc = acc + ext_ref[pl.ds(first + w, tt), :] * cw_ref[w:w + 1, :]
    out_ref[...] = _ln_swish(acc, lg_ref[...], lb_ref[...])

    @pl.when(t == nt - 1)
    def _():
        buf_ref[...] = ext_ref[pl.ds(tt + first, CONV_W - 1), :]

    ext_ref[0:CONV_HIST, :] = ext_ref[tt:tt + CONV_HIST, :]


def conv_branch_prompt(za, cw, cb, lg, lb, batch):
    n = za.shape[0]
    seq = n // batch
    tt = 512
    nt = seq // tt
    return pl.pallas_call(
        functools.partial(_conv_kernel, tt=tt, nt=nt),
        grid=(batch, nt),
        in_specs=[pl.BlockSpec((tt, 2 * D_A), lambda b, t: (b * nt + t, 0)),
                  _const_spec((CONV_W, D_A)), _const_spec((1, D_A)), _const_spec((1, D_A)),
                  _const_spec((1, D_A))],
        out_specs=[pl.BlockSpec((tt, D_A), lambda b, t: (b * nt + t, 0)),
                   pl.BlockSpec((None, CONV_W - 1, D_A), lambda b, t: (b, 0, 0))],
        out_shape=[jax.ShapeDtypeStruct((n, D_A), F32),
                   jax.ShapeDtypeStruct((batch, CONV_W - 1, D_A), F32)],
        scratch_shapes=[pltpu.VMEM((tt + CONV_HIST, D_A), F32)],
        compiler_params=_cparams(("parallel", "arbitrary")),
        name="conv_branch_prompt",
    )(za, cw, cb.reshape(1, D_A), lg.reshape(1, D_A), lb.reshape(1, D_A))


def _log_decay(zg, wg, bg):
    x = _dot(zg.astype(BF16), wg) + bg
    return (jnp.minimum(x, 0.0) - jnp.log1p(jnp.exp(-jnp.abs(x)))) * (1.0 / GATE_TAU)


def _head_norm_gate(o, hg, r):
    outs = []
    for h in range(H_B):
        oh = o[:, h * DV_B:(h + 1) * DV_B]
        outs.append(oh * lax.rsqrt(jnp.mean(oh * oh, axis=-1, keepdims=True) + EPS) * hg)
    return jnp.concatenate(outs, axis=1) * (r * jax.nn.sigmoid(r))


def _gla_kernel(zqkv_ref, zr_ref, zg_ref, wg_ref, bg_ref, hg_ref, ltri_ref, ones_ref,
                bout_ref, sfin_ref, s_ref, *, tt, nt):
    t = pl.program_id(1)

    @pl.when(t == 0)
    def _():
        s_ref[...] = jnp.zeros((D_BK, D_BV), F32)

    zq = zqkv_ref[...]
    q = zq[:, 0:D_BK] * (DK_B ** -0.5)
    k = zq[:, D_BK:2 * D_BK]
    v = zq[:, 2 * D_BK:]
    la = _log_decay(zg_ref[...], wg_ref[...], bg_ref[...])
    la3 = _split3(la)
    ltri = ltri_ref[...]
    b = _dot(ltri, la3[0]) + _dot(ltri, la3[1]) + _dot(ltri, la3[2])
    ones = ones_ref[...]

    lane_head = lax.broadcasted_iota(I32, (GLA_SUB, D_BK), 1) // DK_B
    bd_mask = (lax.broadcasted_iota(I32, (D_BK, D_BV), 0) // DK_B
               == lax.broadcasted_iota(I32, (D_BK, D_BV), 1) // DV_B)
    o_chunks = []
    for c in range(tt // GLA_CHUNK):
        r0 = c * GLA_CHUNK
        bc = b[r0:r0 + GLA_CHUNK]
        qc = q[r0:r0 + GLA_CHUNK]
        kc = k[r0:r0 + GLA_CHUNK]
        vc = v[r0:r0 + GLA_CHUNK]
        vcb = vc.astype(BF16)
        s_prev = s_ref[...]
        o = _dot((qc * jnp.exp(bc)).astype(BF16), s_prev.astype(BF16))
        o_rows = []
        for i in range(GLA_CHUNK // GLA_SUB):
            i0 = i * GLA_SUB
            nk = i0 + GLA_SUB
            ref_b = bc[i0:i0 + 1, :]
            qi = qc[i0:nk] * jnp.exp(bc[i0:nk] - ref_b)
            ki = kc[0:nk] * jnp.exp(ref_b - bc[0:nk])
            qs = jnp.concatenate([jnp.where(lane_head == h, qi, 0.0) for h in range(H_B)], axis=0)
            att = _dot_nt(qs.astype(BF16), ki.astype(BF16))
            row_i = lax.broadcasted_iota(I32, (H_B * GLA_SUB, nk), 0) % GLA_SUB
            col_j = lax.broadcasted_iota(I32, (H_B * GLA_SUB, nk), 1)
            att = jnp.where(col_j <= row_i + i0, att, 0.0)
            res = _dot(att.astype(BF16), vcb[0:nk])
            o_rows.append(jnp.concatenate(
                [res[h * GLA_SUB:(h + 1) * GLA_SUB, h * DV_B:(h + 1) * DV_B] for h in range(H_B)], axis=1))
        o_chunks.append(o + jnp.concatenate(o_rows, axis=0))
        b_end = bc[GLA_CHUNK - 1:GLA_CHUNK, :]
        kst = kc * jnp.exp(b_end - bc)
        kv = _dot_tn(kst.astype(BF16), vcb)
        b_end_t = (_dot_tn(la3[0][r0:r0 + GLA_CHUNK], ones) + _dot_tn(la3[1][r0:r0 + GLA_CHUNK], ones)
                   + _dot_tn(la3[2][r0:r0 + GLA_CHUNK], ones))
        dec = jnp.exp(b_end_t)
        s_ref[...] = s_prev * jnp.concatenate([dec] * H_B, axis=1) + jnp.where(bd_mask, kv, 0.0)

    o_all = jnp.concatenate(o_chunks, axis=0)
    bout_ref[...] = _head_norm_gate(o_all, hg_ref[...], zr_ref[...])

    @pl.when(t == nt - 1)
    def _():
        for h in range(H_B):
            sfin_ref[h] = s_ref[h * DK_B:(h + 1) * DK_B, h * DV_B:(h + 1) * DV_B]


def gla_prompt(zqkv, zr, zg, wg, bg, hg, batch):
    n = zqkv.shape[0]
    seq = n // batch
    tt = 256
    nt = seq // tt
    ri = lax.broadcasted_iota(I32, (tt, tt), 0)
    ci = lax.broadcasted_iota(I32, (tt, tt), 1)
    ltri = ((ri // GLA_CHUNK == ci // GLA_CHUNK) & (ci <= ri)).astype(BF16)
    ones = jnp.ones((GLA_CHUNK, LANES), BF16)
    row = lambda b, t: (b * nt + t, 0)
    return pl.pallas_call(
        functools.partial(_gla_kernel, tt=tt, nt=nt),
        grid=(batch, nt),
        in_specs=[pl.BlockSpec((tt, 2 * D_BK + D_BV), row), pl.BlockSpec((tt, D_BV), row),
                  pl.BlockSpec((tt, LANES), row), _const_spec((LANES, D_BK)), _const_spec((1, D_BK)),
                  _const_spec((1, DV_B)), _const_spec((tt, tt)), _const_spec((GLA_CHUNK, LANES))],
        out_specs=[pl.BlockSpec((tt, D_BV), row),
                   pl.BlockSpec((None, H_B, DK_B, DV_B), lambda b, t: (b, 0, 0, 0))],
        out_shape=[jax.ShapeDtypeStruct((n, D_BV), F32),
                   jax.ShapeDtypeStruct((batch, H_B, DK_B, DV_B), F32)],
        scratch_shapes=[pltpu.VMEM((D_BK, D_BV), F32)],
        compiler_params=_cparams(("parallel", "arbitrary")),
        name="gla_prompt",
    )(zqkv, zr, zg, wg, bg.reshape(1, D_BK), hg.reshape(1, DV_B), ltri, ones)


def _even_sample_a_kernel(za_ref, zqkv_ref, zg_ref, st_ref, cw_ref, cb_ref, lg_ref, lb_ref, wg_ref, bg_ref,
                          u_ref, aout_ref, q_ref, k_ref, a_ref):
    za = za_ref[...]
    u = za[:, :D_A] * jax.nn.sigmoid(za[:, D_A:])
    u_ref[...] = u
    acc = jnp.zeros_like(u) + cb_ref[...]
    for w in range(CONV_W - 1):
        acc = acc + st_ref[w] * cw_ref[w:w + 1, :]
    acc = acc + u * cw_ref[CONV_W - 1:CONV_W, :]
    aout_ref[...] = _ln_swish(acc, lg_ref[...], lb_ref[...])
    zq = zqkv_ref[...]
    q_ref[...] = zq[:, 0:D_BK] * (DK_B ** -0.5)
    k_ref[...] = zq[:, D_BK:2 * D_BK]
    a_ref[...] = jnp.exp(_log_decay(zg_ref[...], wg_ref[...], bg_ref[...]))


def even_sample_a(za, zqkv, zg, st_t, cw, cb, lg, lb, wg, bg):
    n = za.shape[0]
    shp = lambda w: jax.ShapeDtypeStruct((n, w), F32)
    return pl.pallas_call(
        _even_sample_a_kernel,
        out_shape=[shp(D_A), shp(D_A), shp(D_BK), shp(D_BK), shp(D_BK)],
        compiler_params=pltpu.CompilerParams(vmem_limit_bytes=VMEM_LIMIT),
        name="even_sample_a",
    )(za, zqkv, zg, st_t, cw, cb.reshape(1, D_A), lg.reshape(1, D_A), lb.reshape(1, D_A), wg,
      bg.reshape(1, D_BK))


def _even_sample_b_kernel(a_ref, k_ref, q_ref, v_ref, r_ref, s_ref, hg_ref, snew_ref, bout_ref):
    v = v_ref[...]
    outs = []
    for h in range(H_B):
        rows = slice(h * DK_B, (h + 1) * DK_B)
        vh = v[:, h * DV_B:(h + 1) * DV_B]
        s_new = a_ref[rows, :] * s_ref[h] + k_ref[rows, :] * vh
        snew_ref[h] = s_new
        outs.append(jnp.sum(q_ref[rows, :] * s_new, axis=0, keepdims=True))
    bout_ref[...] = _head_norm_gate(jnp.concatenate(outs, axis=1), hg_ref[...], r_ref[...])


def even_sample_b(a3, k3, q3, v3, r3, state, hg):
    nb = a3.shape[0]
    col = pl.BlockSpec((None, D_BK, 1), lambda b: (b, 0, 0))
    rowv = pl.BlockSpec((None, 1, D_BV), lambda b: (b, 0, 0))
    st = pl.BlockSpec((None, H_B, DK_B, DV_B), lambda b: (b, 0, 0, 0))
    return pl.pallas_call(
        _even_sample_b_kernel,
        grid=(nb,),
        in_specs=[col, col, col, rowv, rowv, st, _const_spec((1, DV_B))],
        out_specs=[st, rowv],
        out_shape=[jax.ShapeDtypeStruct((nb, H_B, DK_B, DV_B), F32),
                   jax.ShapeDtypeStruct((nb, 1, D_BV), F32)],
        compiler_params=_cparams(("parallel",)),
        name="even_sample_b",
    )(a3, k3, q3, v3, r3, state, hg.reshape(1, DV_B))


def _even_out_kernel(x_ref, a_ref, b_ref, w_ref, o_ref):
    y = _dot(a_ref[...].astype(BF16), w_ref[0:D_A, :]) + _dot(b_ref[...].astype(BF16), w_ref[D_A:, :])
    o_ref[...] = x_ref[...] + y


def even_out(x, a, b, w):
    n = x.shape[0]
    tm = min(512, n)
    row = lambda i: (i, 0)
    return pl.pallas_call(
        _even_out_kernel,
        grid=(n // tm,),
        in_specs=[pl.BlockSpec((tm, D_MODEL), row), pl.BlockSpec((tm, D_A), row),
                  pl.BlockSpec((tm, D_BV), row), _const_spec(w.shape)],
        out_specs=pl.BlockSpec((tm, D_MODEL), row),
        out_shape=jax.ShapeDtypeStruct((n, D_MODEL), F32),
        compiler_params=_cparams(("parallel",)),
        name="even_out",
    )(x, a, b, w)


def _mla_prep_kernel(cq_ref, ckv_ref, kpe_ref, kper_ref, cs_ref, sn_ref, qg_ref, kvg_ref,
                     wn_ref, wa_ref, wb_ref, wuk_ref, qc_ref, kc_ref, ckvo_ref, kpeo_ref):
    cs = cs_ref[...]
    sn = sn_ref[...]
    ckv = _rms(ckv_ref[...], kvg_ref[...])
    kpe = kpe_ref[...] * cs + kper_ref[...] * sn
    ckvo_ref[...] = ckv
    kpeo_ref[...] = kpe
    kc_ref[:, 0:R_KV] = ckv.astype(BF16)
    kc_ref[:, R_KV:] = kpe.astype(BF16)
    cqn = _rms(cq_ref[...], qg_ref[...]).astype(BF16)
    qn = _dot(cqn, wn_ref[...]).astype(BF16)
    qa = _dot(cqn, wa_ref[...])
    qb = _dot(cqn, wb_ref[...])
    for h in range(H_C):
        pair = qn[:, (h // 2) * LANES:(h // 2 + 1) * LANES]
        ql = _dot(pair, wuk_ref[h]) * MLA_SCALE
        qp = (qa[:, h * LANES:(h + 1) * LANES] * cs + qb[:, h * LANES:(h + 1) * LANES] * sn) * MLA_SCALE
        qc_ref[h, :, 0:R_KV] = ql.astype(BF16)
        qc_ref[h, :, R_KV:] = qp.astype(BF16)


def mla_prep(cq, ckv, kpe, kper, cs, sn, qg, kvg, wn, wa, wb, wuk, pos_tiles):
    n = cq.shape[0]
    tm = ATT_TQ
    row = lambda i: (i, 0)
    pos = lambda i: (i % pos_tiles, 0)
    return pl.pallas_call(
        _mla_prep_kernel,
        grid=(n // tm,),
        in_specs=[pl.BlockSpec((tm, R_Q), row), pl.BlockSpec((tm, R_KV), row),
                  pl.BlockSpec((tm, LANES), row), pl.BlockSpec((tm, LANES), row),
                  pl.BlockSpec((tm, LANES), pos), pl.BlockSpec((tm, LANES), pos),
                  _const_spec((1, R_Q)), _const_spec((1, R_KV)),
                  _const_spec(wn.shape), _const_spec(wa.shape), _const_spec(wb.shape), _const_spec(wuk.shape)],
        out_specs=[pl.BlockSpec((None, H_C, tm, KC_W), lambda i: (i, 0, 0, 0)),
                   pl.BlockSpec((tm, KC_W), row), pl.BlockSpec((tm, R_KV), row),
                   pl.BlockSpec((tm, LANES), row)],
        out_shape=[jax.ShapeDtypeStruct((n // tm, H_C, tm, KC_W), BF16),
                   jax.ShapeDtypeStruct((n, KC_W), BF16),
                   jax.ShapeDtypeStruct((n, R_KV), F32),
                   jax.ShapeDtypeStruct((n, LANES), F32)],
        compiler_params=_cparams(("parallel",)),
        name="mla_prep",
    )(cq, ckv, kpe, kper, cs, sn, qg.reshape(1, R_Q), kvg.reshape(1, R_KV), wn, wa, wb, wuk)


def _attn_update(q, kc, visible, m_ref, l_ref, acc_ref):
    s = _dot_nt(q, kc)
    if visible is not None:
        s = jnp.where(visible, s, NEG_INF)
    m_old = m_ref[...]
    m_new = jnp.maximum(m_old, jnp.max(s, axis=1, keepdims=True))
    p = jnp.exp(s - m_new)
    alpha = jnp.exp(m_old - m_new)
    l_ref[...] = alpha * l_ref[...] + jnp.sum(p, axis=1, keepdims=True)
    acc_ref[...] = alpha * acc_ref[...] + _dot(p.astype(BF16), kc[:, 0:R_KV])
    m_ref[...] = m_new


def _mla_attn_kernel(qc_ref, kc_ref, ol_ref, m_ref, l_ref, acc_ref):
    i = pl.program_id(1)
    rows = H_C * ATT_TQ
    m_ref[...] = jnp.full((rows, 1), NEG_INF, F32)
    l_ref[...] = jnp.zeros((rows, 1), F32)
    acc_ref[...] = jnp.zeros((rows, R_KV), F32)
    q = qc_ref[...].reshape(rows, KC_W)
    n_full = (i * ATT_TQ) // ATT_TK

    def full_block(j, carry):
        _attn_update(q, kc_ref[pl.ds(pl.multiple_of(j * ATT_TK, ATT_TK), ATT_TK), :], None,
                     m_ref, l_ref, acc_ref)
        return carry

    lax.fori_loop(0, n_full, full_block, 0)
    k0 = pl.multiple_of(n_full * ATT_TK, ATT_TK)
    tok = i * ATT_TQ + lax.broadcasted_iota(I32, (rows, ATT_TK), 0) % ATT_TQ
    key = k0 + lax.broadcasted_iota(I32, (rows, ATT_TK), 1)
    _attn_update(q, kc_ref[pl.ds(k0, ATT_TK), :], key <= tok, m_ref, l_ref, acc_ref)
    ol_ref[...] = (acc_ref[...] / l_ref[...]).reshape(H_C, ATT_TQ, R_KV).astype(BF16)


def mla_attn_prompt(qc, kc, batch):
    ntile = qc.shape[0]
    nq = ntile // batch
    seq = nq * ATT_TQ
    rows = H_C * ATT_TQ
    return pl.pallas_call(
        _mla_attn_kernel,
        grid=(batch, nq),
        in_specs=[pl.BlockSpec((None, H_C, ATT_TQ, KC_W), lambda b, i: (b * nq + i, 0, 0, 0)),
                  pl.BlockSpec((seq, KC_W), lambda b, i: (b, 0))],
        out_specs=pl.BlockSpec((None, H_C, ATT_TQ, R_KV), lambda b, i: (b * nq + i, 0, 0, 0)),
        out_shape=jax.ShapeDtypeStruct((ntile, H_C, ATT_TQ, R_KV), BF16),
        scratch_shapes=[pltpu.VMEM((rows, 1), F32), pltpu.VMEM((rows, 1), F32),
                        pltpu.VMEM((rows, R_KV), F32)],
        compiler_params=_cparams(("parallel", "arbitrary")),
        name="mla_attn_prompt",
    )(qc, kc)


def _mla_decode_kernel(pt_ref, q_ref, kcur_ref, *refs, ng):
    ckv_refs = refs[:DEC_PAGES]
    kpe_refs = refs[DEC_PAGES:2 * DEC_PAGES]
    o_ref, m_ref, l_ref, acc_ref = refs[2 * DEC_PAGES:]
    g = pl.program_id(1)

    @pl.when(g == 0)
    def _():
        m_ref[...] = jnp.full((H_C, 1), NEG_INF, F32)
        l_ref[...] = jnp.zeros((H_C, 1), F32)
        acc_ref[...] = jnp.zeros((H_C, R_KV), F32)

    q = q_ref[...]
    ql = q[:, 0:R_KV]
    qp = q[:, R_KV:R_KV + D_ROPE]
    cks = [r[...].astype(BF16) for r in ckv_refs]
    s = jnp.concatenate(
        [_dot_nt(ql, cks[p]) + _dot_nt(qp, kpe_refs[p][...].astype(BF16)) for p in range(DEC_PAGES)], axis=1)
    m_old = m_ref[...]
    m_new = jnp.maximum(m_old, jnp.max(s, axis=1, keepdims=True))
    p_all = jnp.exp(s - m_new).astype(BF16)
    alpha = jnp.exp(m_old - m_new)
    pv = _dot(p_all[:, 0:PAGE_SIZE], cks[0])
    for p in range(1, DEC_PAGES):
        pv = pv + _dot(p_all[:, p * PAGE_SIZE:(p + 1) * PAGE_SIZE], cks[p])
    l_new = alpha * l_ref[...] + jnp.sum(p_all.astype(F32), axis=1, keepdims=True)
    acc_new = alpha * acc_ref[...] + pv
    m_ref[...] = m_new
    l_ref[...] = l_new
    acc_ref[...] = acc_new

    @pl.when(g == ng - 1)
    def _():
        kcur = kcur_ref[...].astype(F32)
        s_cur = jnp.sum(q.astype(F32) * kcur, axis=1, keepdims=True)
        m_fin = jnp.maximum(m_new, s_cur)
        a_fin = jnp.exp(m_new - m_fin)
        p_cur = jnp.exp(s_cur - m_fin)
        l_fin = a_fin * l_new + p_cur
        o_ref[...] = (a_fin * acc_new + p_cur * kcur[:, 0:R_KV]) / l_fin


def mla_decode(page_table, q_s, kc_s, cache_ckv, cache_kpe, layer):
    nb, npages = page_table.shape
    ng = npages // DEC_PAGES

    def page_spec(width, p):
        return pl.BlockSpec((None, None, PAGE_SIZE, width),
                            lambda b, g, pt: (layer, pt[b, g * DEC_PAGES + p], 0, 0))

    grid_spec = pltpu.PrefetchScalarGridSpec(
        num_scalar_prefetch=1,
        grid=(nb, ng),
        in_specs=([pl.BlockSpec((None, H_C, KC_W), lambda b, g, pt: (b, 0, 0)),
                   pl.BlockSpec((None, 1, KC_W), lambda b, g, pt: (b, 0, 0))]
                  + [page_spec(R_KV, p) for p in range(DEC_PAGES)]
                  + [page_spec(D_ROPE, p) for p in range(DEC_PAGES)]),
        out_specs=pl.BlockSpec((None, H_C, R_KV), lambda b, g, pt: (b, 0, 0)),
        scratch_shapes=[pltpu.VMEM((H_C, 1), F32), pltpu.VMEM((H_C, 1), F32), pltpu.VMEM((H_C, R_KV), F32)],
    )
    return pl.pallas_call(
        functools.partial(_mla_decode_kernel, ng=ng),
        grid_spec=grid_spec,
        out_shape=jax.ShapeDtypeStruct((nb, H_C, R_KV), F32),
        compiler_params=_cparams(("parallel", "arbitrary")),
        name="mla_decode",
    )(page_table, q_s, kc_s, *([cache_ckv] * DEC_PAGES), *([cache_kpe] * DEC_PAGES))


def _mla_out_kernel(x_ref, ol_ref, wuv_ref, wo_ref, o_ref):
    pairs = []
    for p in range(H_C // 2):
        pairs.append((_dot(ol_ref[2 * p], wuv_ref[2 * p]) + _dot(ol_ref[2 * p + 1], wuv_ref[2 * p + 1])).astype(BF16))
    o = jnp.concatenate(pairs, axis=1)
    o_ref[...] = x_ref[...] + _dot(o, wo_ref[...])


def mla_out(x, ol, wuv, wo):
    n = x.shape[0]
    tm = ATT_TQ
    return pl.pallas_call(
        _mla_out_kernel,
        grid=(n // tm,),
        in_specs=[pl.BlockSpec((tm, D_MODEL), lambda i: (i, 0)),
                  pl.BlockSpec((None, H_C, tm, R_KV), lambda i: (i, 0, 0, 0)),
                  _const_spec(wuv.shape), _const_spec(wo.shape)],
        out_specs=pl.BlockSpec((tm, D_MODEL), lambda i: (i, 0)),
        out_shape=jax.ShapeDtypeStruct((n, D_MODEL), F32),
        compiler_params=_cparams(("parallel",)),
        name="mla_out",
    )(x, ol, wuv, wo)


def _top16_rows(s):
    rows, tm = s.shape
    iota = lax.broadcasted_iota(I32, (rows, tm), 0)
    rank = jnp.full((rows, tm), TOPK, I32)
    vals = []
    cur = s
    for it in range(TOPK):
        m = jnp.max(cur, axis=0, keepdims=True)
        idx = jnp.min(jnp.where(cur == m, iota, rows), axis=0, keepdims=True)
        sel = iota == idx
        rank = jnp.where(sel, it, rank)
        cur = jnp.where(sel, NEG_INF, cur)
        vals.append(m)
    return jnp.concatenate(vals, axis=0), rank


CAND_ROWS = 80


def _pair_top16(v1, v2):
    cand, _, flat = _pair_candidates(v1, v2)
    sums, a_sel = [], []
    for _ in range(TOPK):
        m = jnp.max(cand, axis=0, keepdims=True)
        ci = jnp.min(jnp.where(cand == m, flat, TOPK * TOPK), axis=0, keepdims=True)
        cand = jnp.where(flat == ci, NEG_INF, cand)
        sums.append(m)
        a_sel.append(ci >> 4)
    return sums, a_sel


def _top16_rows_distinct(s):
    rows, tm = s.shape
    rank = jnp.full((rows, tm), float(TOPK), F32)
    vals = []
    cur = s
    for it in range(TOPK):
        m = jnp.max(cur, axis=0, keepdims=True)
        sel = cur == m
        rank = jnp.where(sel, float(it), rank)
        cur = jnp.where(sel, NEG_INF, cur)
        vals.append(m)
    marked = jnp.sum(jnp.where(rank < float(TOPK), 1.0, 0.0), axis=0, keepdims=True)
    return jnp.concatenate(vals, axis=0), rank, marked


def _pair_candidates(v1, v2):
    tm = v1.shape[1]
    pieces = [v1[0:1] + v2]
    for a in range(1, 8):
        pieces.append(v1[a:a + 1] + v2[0:8])
    pieces.append(v1[8:16] + v2[0:1])
    cand = jnp.concatenate(pieces, axis=0)
    r = lax.broadcasted_iota(I32, (CAND_ROWS, tm), 0)
    a_idx = jnp.where(r < 16, 0, jnp.where(r < 72, 1 + ((r - 16) >> 3), r - 64))
    b_idx = jnp.where(r < 16, r, jnp.where(r < 72, (r - 16) & 7, 0))
    cand = jnp.where((a_idx + 1) * (b_idx + 1) <= TOPK, cand, NEG_INF)
    return cand, a_idx, a_idx * TOPK + b_idx


def _pair_top16_distinct(v1, v2):
    cand, _, _ = _pair_candidates(v1, v2)
    cur = cand
    sums = []
    for _ in range(TOPK):
        m = jnp.max(cur, axis=0, keepdims=True)
        cur = jnp.where(cur == m, NEG_INF, cur)
        sums.append(m)
    picked = jnp.where((cur == NEG_INF) & (cand > NEG_INF), 1.0, 0.0)
    counts = [jnp.sum(picked[0:16], axis=0, keepdims=True)]
    for a in range(1, 8):
        counts.append(jnp.sum(picked[8 + 8 * a:16 + 8 * a], axis=0, keepdims=True))
    cnt = jnp.concatenate(counts + [picked[72:80]], axis=0)
    return sums, cnt, jnp.sum(picked, axis=0, keepdims=True)


def _route_tables(s1, s2, rank1, rank2, v1, v2, sums, cnt, refs, h):
    r2_ref, bt_ref, e1_ref, e2_ref = refs
    z = jnp.exp(sums[0] - sums[0])
    for kk in range(1, TOPK):
        z = z + jnp.exp(sums[kk] - sums[0])
    bt = jnp.zeros(s1.shape, F32)
    for a in range(TOPK):
        bt = jnp.where(rank1 == a, cnt[a:a + 1], bt)
    e1 = jnp.exp(s1 - v1[0:1]) * (1.0 / z)
    e2 = jnp.exp(s2 - v2[0:1])
    r2 = rank2.astype(F32)
    for c in range(s1.shape[1] // LANES):
        cols = slice(c * LANES, (c + 1) * LANES)
        bt_ref[h, c] = bt[:, cols]
        e1_ref[h, c] = e1[:, cols]
        e2_ref[h, c] = e2[:, cols].astype(BF16)
        r2_ref[h, c] = r2[:, cols].astype(BF16)


def _peer_route_kernel(x_ref, g_ref, wq_ref, keys_ref, xn_ref, r2_ref, bt_ref, e1_ref, e2_ref, q_ref):
    xn = _rms(x_ref[...], g_ref[...]).astype(BF16)
    xn_ref[...] = xn
    q_ref[...] = _dot(xn, wq_ref[...]).astype(BF16)
    tm = q_ref.shape[0]
    refs = (r2_ref, bt_ref, e1_ref, e2_ref)

    def scores(h):
        s1 = _dot_nt(keys_ref[h, 0], q_ref[:, (2 * h) * D_HALF:(2 * h + 1) * D_HALF])
        s2 = _dot_nt(keys_ref[h, 1], q_ref[:, (2 * h + 1) * D_HALF:(2 * h + 2) * D_HALF])
        return s1, s2

    bad = jnp.zeros((1, tm), F32)
    for h in range(H_P):
        s1, s2 = scores(h)
        v1, rank1, n1 = _top16_rows_distinct(s1)
        v2, rank2, n2 = _top16_rows_distinct(s2)
        sums, cnt, n3 = _pair_top16_distinct(v1, v2)
        _route_tables(s1, s2, rank1, rank2, v1, v2, sums, cnt, refs, h)
        bad = bad + jnp.abs(n1 - TOPK) + jnp.abs(n2 - TOPK) + jnp.abs(n3 - TOPK)

    @pl.when(jnp.max(bad) > 0.0)
    def _():
        iota16 = lax.broadcasted_iota(I32, (TOPK, tm), 0)
        for h in range(H_P):
            s1, s2 = scores(h)
            v1, rank1 = _top16_rows(s1)
            v2, rank2 = _top16_rows(s2)
            sums, a_sel = _pair_top16(v1, v2)
            cnt = jnp.zeros((TOPK, tm), F32)
            for kk in range(TOPK):
                cnt = cnt + jnp.where(iota16 == a_sel[kk], 1.0, 0.0)
            _route_tables(s1, s2, rank1, rank2, v1, v2, sums, cnt, refs, h)


def peer_route(x, g, wq, keys):
    n = x.shape[0]
    tm = min(256, n)
    tab = pl.BlockSpec((H_P, tm // LANES, N_KEYS, LANES), lambda i: (0, i, 0, 0))
    tab_shape = lambda dt: jax.ShapeDtypeStruct((H_P, n // LANES, N_KEYS, LANES), dt)
    return pl.pallas_call(
        _peer_route_kernel,
        grid=(n // tm,),
        in_specs=[pl.BlockSpec((tm, D_MODEL), lambda i: (i, 0)), _const_spec((1, D_MODEL)),
                  _const_spec(wq.shape), _const_spec(keys.shape)],
        out_specs=[pl.BlockSpec((tm, D_MODEL), lambda i: (i, 0)), tab, tab, tab, tab],
        out_shape=[jax.ShapeDtypeStruct((n, D_MODEL), BF16), tab_shape(BF16), tab_shape(F32), tab_shape(F32),
                   tab_shape(BF16)],
        scratch_shapes=[pltpu.VMEM((tm, H_P * 2 * D_HALF), BF16)],
        compiler_params=_cparams(("parallel",)),
        name="peer_route",
    )(x, g.reshape(1, D_MODEL), wq, keys)


def _peer_dense_kernel(x_ref, xn_ref, r2_ref, bt_ref, e1_ref, e2_ref, u_ref, vt_ref, o_ref,
                       acc_ref, ht_ref, act_ref, *, tm, te, ne):
    e = pl.program_id(1)

    @pl.when(e == 0)
    def _():
        acc_ref[...] = jnp.zeros_like(acc_ref)

    assert te == SUBLANES * N_KEYS
    ht_ref[...] = _dot_nt(u_ref[...], xn_ref[...])
    grp = N_KEYS // BF16_ROWS

    def token_row(ref, h, c, ii):
        return jnp.broadcast_to(ref[h, c, e, ii:ii + 1, :], (BF16_ROWS, LANES)).astype(BF16)[None]

    for pair in range(SUBLANES // 2):
        iis = (2 * pair, 2 * pair + 1)
        for c in range(tm // LANES):
            cols = slice(c * LANES, (c + 1) * LANES)
            gates = [jnp.zeros((grp, BF16_ROWS, LANES), BF16) for _ in iis]
            for h in range(H_P):
                r2 = r2_ref[h, c]
                e2 = e2_ref[h, c]
                for k, ii in enumerate(iis):
                    val = e2 * token_row(e1_ref, h, c, ii)
                    gates[k] = gates[k] + jnp.where(r2 < token_row(bt_ref, h, c, ii), val, jnp.zeros_like(val))
            for k, ii in enumerate(iis):
                rows = slice(ii * N_KEYS, (ii + 1) * N_KEYS)
                ht = ht_ref[rows, cols]
                gelu = (0.5 * ht * (1.0 + lax.erf(ht * (2.0 ** -0.5)))).astype(BF16)
                act_ref[rows, cols] = (gelu.reshape(grp, BF16_ROWS, LANES) * gates[k]).reshape(N_KEYS, LANES)
    acc_ref[...] += _dot(vt_ref[...], act_ref[...])

    @pl.when(e == ne - 1)
    def _():
        o_ref[...] = x_ref[...] + acc_ref[...].T


def peer_dense(x, xn, r2, bt, e1, e2, u, vt):
    n = x.shape[0]
    tm = min(512, n)
    te = SUBLANES * N_KEYS
    ne = N_EXPERTS // te
    f32_view = (H_P, n // LANES, N_KEYS // SUBLANES, SUBLANES, LANES)
    bf16_view = (H_P, n // LANES, N_KEYS // BF16_ROWS, BF16_ROWS, LANES)
    tab = lambda view: pl.BlockSpec((H_P, tm // LANES) + view[2:], lambda i, e: (0, i, 0, 0, 0))
    return pl.pallas_call(
        functools.partial(_peer_dense_kernel, tm=tm, te=te, ne=ne),
        grid=(n // tm, ne),
        in_specs=[pl.BlockSpec((tm, D_MODEL), lambda i, e: (i, 0)),
                  pl.BlockSpec((tm, D_MODEL), lambda i, e: (i, 0)),
                  tab(bf16_view), tab(f32_view), tab(f32_view), tab(bf16_view),
                  pl.BlockSpec((te, D_MODEL), lambda i, e: (e, 0)),
                  pl.BlockSpec((D_MODEL, te), lambda i, e: (0, e))],
        out_specs=pl.BlockSpec((tm, D_MODEL), lambda i, e: (i, 0)),
        out_shape=jax.ShapeDtypeStruct((n, D_MODEL), F32),
        scratch_shapes=[pltpu.VMEM((D_MODEL, tm), F32), pltpu.VMEM((te, tm), F32),
                        pltpu.VMEM((te, tm), BF16)],
        compiler_params=_cparams(("parallel", "arbitrary")),
        name="peer_dense",
    )(x, xn, r2.reshape(bf16_view), bt.reshape(f32_view), e1.reshape(f32_view), e2.reshape(bf16_view), u, vt)


def peer(x, g, wq, keys, u, vt):
    xn, r2, bt, e1, e2 = peer_route(x, g, wq, keys)
    return peer_dense(x, xn, r2, bt, e1, e2, u, vt)


def _pad_cols(w, width):
    return jnp.pad(w, ((0, 0), (0, width - w.shape[1])))


def _rope_tables(pos):
    half = D_ROPE // 2
    inv = jnp.power(ROPE_BASE, -jnp.arange(half, dtype=F32) / half)
    ang = pos.astype(F32)[:, None] * inv[None, :]
    cos2 = jnp.concatenate([jnp.cos(ang), jnp.cos(ang)], axis=1)
    sin2 = jnp.concatenate([jnp.sin(ang), jnp.sin(ang)], axis=1)
    return _pad_cols(cos2, LANES), _pad_cols(sin2, LANES)


def _rot_half_cols(w):
    half = D_ROPE // 2
    return jnp.concatenate([-w[..., half:], w[..., :half]], axis=-1)


def _even_layer_weights(w_in, w_gate, w_out):
    w_in_p = _pad_cols(w_in, 2 * D_A + 2 * D_BK + 2 * D_BV + LANES).astype(BF16)
    wg = jnp.pad(w_gate, ((0, LANES - GATE_RANK), (0, 0))).astype(BF16)
    return w_in_p, wg, w_out.astype(BF16)


def _odd_layer_weights(w_in, w_q_b, w_uk, w_uv, w_out):
    wk = w_in[:, R_Q + R_KV:]
    w_in_p = jnp.concatenate([w_in[:, :R_Q + R_KV], _pad_cols(wk, LANES), _pad_cols(_rot_half_cols(wk), LANES)],
                             axis=1).astype(BF16)
    wq3 = w_q_b.reshape(R_Q, H_C, D_NOPE + D_ROPE)
    wn = wq3[:, :, :D_NOPE].reshape(R_Q, H_C * D_NOPE).astype(BF16)
    wpe = wq3[:, :, D_NOPE:]
    pad3 = lambda w: jnp.pad(w, ((0, 0), (0, 0), (0, LANES - D_ROPE))).reshape(R_Q, H_C * LANES).astype(BF16)
    wa, wb = pad3(wpe), pad3(_rot_half_cols(wpe))
    ukt = jnp.transpose(w_uk, (1, 2, 0))
    uv = jnp.transpose(w_uv, (1, 0, 2))
    odd = (jnp.arange(H_C) % 2 == 1)[:, None, None]
    zk = jnp.zeros_like(ukt)
    wuk = jnp.where(odd, jnp.concatenate([zk, ukt], axis=1), jnp.concatenate([ukt, zk], axis=1)).astype(BF16)
    zv = jnp.zeros_like(uv)
    wuv = jnp.where(odd, jnp.concatenate([zv, uv], axis=2), jnp.concatenate([uv, zv], axis=2)).astype(BF16)
    return w_in_p, wn, wa, wb, wuk, wuv, w_out.astype(BF16)


def kernel(x_prompt, x_sample, cache_ckv, cache_kpe, state_conv, state_gla, page_table, ab_norm, ab_w_in,
           conv_w, conv_b, conv_ln_g, conv_ln_b, gla_w_gate, gla_b_gate, gla_head_g, ab_w_out, c_norm, c_w_in,
           c_q_norm, c_kv_norm, c_w_q_b, c_w_uk, c_w_uv, c_w_out, ffn_norm, peer_w_q, peer_keys, peer_u,
           peer_v, final_norm):
    bp, tp, _ = x_prompt.shape
    bs = x_sample.shape[0]
    n_p = bp * tp
    xp = x_prompt.reshape(n_p, D_MODEL)
    xs = jnp.pad(x_sample.reshape(bs, D_MODEL), ((0, SAMPLE_ROWS - bs), (0, 0)))
    cs_p, sn_p = _rope_tables(jnp.arange(tp, dtype=I32))
    cs_s, sn_s = _rope_tables(jnp.full((SAMPLE_ROWS,), PAST_LEN, I32))
    ckv_p, kpe_p, ckv_s, kpe_s = [], [], [], []
    conv_p, conv_s, gla_p, gla_s = [], [], [], []
    for l in range(DEPTH):
        j = l // 2
        if l % 2 == 0:
            w_in, wg, w_out = _even_layer_weights(ab_w_in[j], gla_w_gate[j], ab_w_out[j])
            widths = (2 * D_A, 2 * D_BK + D_BV, D_BV, LANES)
            za, zqkv, zr, zg = norm_matmul(xp, ab_norm[j], w_in, widths)
            a_out, cbuf = conv_branch_prompt(za, conv_w[j], conv_b[j], conv_ln_g[j], conv_ln_b[j], bp)
            b_out, s_fin = gla_prompt(zqkv, zr, zg, wg, gla_b_gate[j], gla_head_g[j], bp)
            xp = even_out(xp, a_out, b_out, w_out)
            conv_p.append(cbuf)
            gla_p.append(s_fin)
            za, zqkv, zr, zg = norm_matmul(xs, ab_norm[j], w_in, widths)
            st_t = jnp.pad(jnp.transpose(state_conv[j], (1, 0, 2)), ((0, 0), (0, SAMPLE_ROWS - bs), (0, 0)))
            u, a_out, qg, kg, ag = even_sample_a(za, zqkv, zg, st_t, conv_w[j], conv_b[j], conv_ln_g[j],
                                                 conv_ln_b[j], wg, gla_b_gate[j])
            col = lambda t: t[:bs].reshape(bs, D_BK, 1)
            s_new, b_out = even_sample_b(col(ag), col(kg), col(qg),
                                         zqkv[:bs, 2 * D_BK:].reshape(bs, 1, D_BV),
                                         zr[:bs].reshape(bs, 1, D_BV), state_gla[j], gla_head_g[j])
            b_out = jnp.pad(b_out.reshape(bs, D_BV), ((0, SAMPLE_ROWS - bs), (0, 0)))
            xs = even_out(xs, a_out, b_out, w_out)
            conv_s.append(jnp.concatenate([state_conv[j][:, 1:], u[:bs, None, :]], axis=1))
            gla_s.append(s_new)
        else:
            w_in, wn, wa, wb, wuk, wuv, w_out = _odd_layer_weights(c_w_in[j], c_w_q_b[j], c_w_uk[j], c_w_uv[j],
                                                                   c_w_out[j])
            widths = (R_Q, R_KV, LANES, LANES)
            cq, ckv, kpe, kper = norm_matmul(xp, c_norm[j], w_in, widths)
            qc, kc, ckv_o, kpe_o = mla_prep(cq, ckv, kpe, kper, cs_p, sn_p, c_q_norm[j], c_kv_norm[j],
                                            wn, wa, wb, wuk, tp // ATT_TQ)
            ol = mla_attn_prompt(qc, kc, bp)
            xp = mla_out(xp, ol, wuv, w_out)
            ckv_p.append(ckv_o.reshape(bp, tp, R_KV))
            kpe_p.append(kpe_o[:, :D_ROPE].reshape(bp, tp, D_ROPE))
            cq, ckv, kpe, kper = norm_matmul(xs, c_norm[j], w_in, widths)
            qc, kc, ckv_o, kpe_o = mla_prep(cq, ckv, kpe, kper, cs_s, sn_s, c_q_norm[j], c_kv_norm[j],
                                            wn, wa, wb, wuk, 1)
            q_s = jnp.transpose(qc[0, :, :bs, :], (1, 0, 2))
            o_lat = mla_decode(page_table, q_s, kc[:bs].reshape(bs, 1, KC_W), cache_ckv, cache_kpe, j)
            ol = jnp.pad(jnp.transpose(o_lat, (1, 0, 2)), ((0, 0), (0, SAMPLE_ROWS - bs), (0, 0)))
            xs = mla_out(xs, ol.astype(BF16)[None], wuv, w_out)
            ckv_s.append(ckv_o[:bs].reshape(bs, 1, R_KV))
            kpe_s.append(kpe_o[:bs, :D_ROPE].reshape(bs, 1, D_ROPE))
        wq = peer_w_q[l].astype(BF16)
        keys = peer_keys[l].astype(BF16)
        u_tab = peer_u[l].astype(BF16)
        vt_tab = peer_v[l].astype(BF16).T
        xp = peer(xp, ffn_norm[l], wq, keys, u_tab, vt_tab)
        xs = peer(xs, ffn_norm[l], wq, keys, u_tab, vt_tab)
    y_prompt = final_rms(xp, final_norm).reshape(bp, tp, D_MODEL)
    y_sample = final_rms(xs, final_norm)[:bs].reshape(bs, 1, D_MODEL)
    return (y_prompt, y_sample,
            jnp.stack(ckv_p), jnp.stack(kpe_p), jnp.stack(ckv_s), jnp.stack(kpe_s),
            jnp.stack(conv_p), jnp.stack(conv_s), jnp.stack(gla_p), jnp.stack(gla_s))
```

```python
import functools

import jax
import jax.numpy as jnp
from jax import lax
from jax.experimental import pallas as pl
from jax.experimental.pallas import tpu as pltpu

F32 = jnp.float32
BF16 = jnp.bfloat16
I32 = jnp.int32

D_MODEL = 1024
SEQ = 2048
DEPTH = 4
DEC_BATCH = 32
PAST_LEN = 16384
PAGE_SIZE = 128
EPS = 1e-6
D_A = 512
CONV_W = 31
H_B = 4
D_BV = 512
D_BK = 256
DK_B = 64
DV_B = 128
GATE_RANK = 16
GATE_TAU = 16.0
GLA_CHUNK = 64
GLA_SUB = 16
H_C = 16
D_NOPE = 64
D_ROPE = 32
D_V = 64
R_Q = 384
R_KV = 256
ROPE_BASE = 10000.0
MLA_SCALE = (D_NOPE + D_ROPE) ** -0.5
H_P = 8
N_KEYS = 128
N_EXPERTS = N_KEYS * N_KEYS
D_HALF = 128
TOPK = 16

LANES = 128
SUBLANES = 8
SAMPLE_ROWS = 128
KC_W = R_KV + LANES
ATT_TQ = 128
ATT_TK = 512
DEC_PAGES = 8
VMEM_LIMIT = 56 * 1024 * 1024
NEG_INF = float("-inf")


def _cparams(sem):
    return pltpu.CompilerParams(dimension_semantics=sem, vmem_limit_bytes=VMEM_LIMIT)


def _dot(a, b):
    return jnp.dot(a, b, preferred_element_type=F32)


def _dot_nt(a, b):
    return lax.dot_general(a, b, (((1,), (1,)), ((), ())), preferred_element_type=F32)


def _dot_tn(a, b):
    return lax.dot_general(a, b, (((0,), (0,)), ((), ())), preferred_element_type=F32)


def _rms(x, g):
    return x * lax.rsqrt(jnp.mean(x * x, axis=-1, keepdims=True) + EPS) * g


def _split3(x):
    hi = x.astype(BF16)
    r1 = x - hi.astype(F32)
    mid = r1.astype(BF16)
    lo = (r1 - mid.astype(F32)).astype(BF16)
    return hi, mid, lo


def _const_spec(shape):
    nd = len(shape)
    return pl.BlockSpec(shape, lambda *_: (0,) * nd)


def _norm_mm_kernel(x_ref, g_ref, w_ref, *out_refs, widths):
    xn = _rms(x_ref[...], g_ref[...])
    y = _dot(xn.astype(BF16), w_ref[...])
    off = 0
    for o_ref, w in zip(out_refs, widths):
        o_ref[...] = y[:, off:off + w]
        off += w


def norm_matmul(x, g, w, widths):
    n, k = x.shape
    tm = min(512, n)
    return pl.pallas_call(
        functools.partial(_norm_mm_kernel, widths=widths),
        grid=(n // tm,),
        in_specs=[pl.BlockSpec((tm, k), lambda i: (i, 0)), _const_spec((1, k)), _const_spec(w.shape)],
        out_specs=[pl.BlockSpec((tm, wd), lambda i: (i, 0)) for wd in widths],
        out_shape=[jax.ShapeDtypeStruct((n, wd), F32) for wd in widths],
        compiler_params=_cparams(("parallel",)),
        name="norm_matmul",
    )(x, g.reshape(1, k), w)


def _final_norm_kernel(x_ref, g_ref, o_ref):
    o_ref[...] = _rms(x_ref[...], g_ref[...])


def final_rms(x, g):
    n, k = x.shape
    tm = min(512, n)
    return pl.pallas_call(
        _final_norm_kernel,
        grid=(n // tm,),
        in_specs=[pl.BlockSpec((tm, k), lambda i: (i, 0)), _const_spec((1, k))],
        out_specs=pl.BlockSpec((tm, k), lambda i: (i, 0)),
        out_shape=jax.ShapeDtypeStruct((n, k), F32),
        compiler_params=_cparams(("parallel",)),
        name="final_norm",
    )(x, g.reshape(1, k))


def _ln_swish(c, g, b):
    mu = jnp.mean(c, axis=-1, keepdims=True)
    xc = c - mu
    y = xc * lax.rsqrt(jnp.mean(xc * xc, axis=-1, keepdims=True) + EPS) * g + b
    return y * jax.nn.sigmoid(y)


CONV_HIST = 32


def _conv_kernel(za_ref, cw_ref, cb_ref, lg_ref, lb_ref, out_ref, buf_ref, ext_ref, *, tt, nt):
    t = pl.program_id(1)

    @pl.when(t == 0)
    def _():
        ext_ref[0:CONV_HIST, :] = jnp.zeros((CONV_HIST, D_A), F32)

    za = za_ref[...]
    ext_ref[CONV_HIST:CONV_HIST + tt, :] = za[:, :D_A] * jax.nn.sigmoid(za[:, D_A:])
    first = CONV_HIST - (CONV_W - 1)
    acc = jnp.zeros((tt, D_A), F32) + cb_ref[...]
    for w in range(CONV_W):
        acc = acc + ext_ref[pl.ds(first + w, tt), :] * cw_ref[w:w + 1, :]
    out_ref[...] = _ln_swish(acc, lg_ref[...], lb_ref[...])

    @pl.when(t == nt - 1)
    def _():
        buf_ref[...] = ext_ref[pl.ds(tt + first, CONV_W - 1), :]

    ext_ref[0:CONV_HIST, :] = ext_ref[tt:tt + CONV_HIST, :]


def conv_branch_prompt(za, cw, cb, lg, lb, batch):
    n = za.shape[0]
    seq = n // batch
    tt = 512
    nt = seq // tt
    return pl.pallas_call(
        functools.partial(_conv_kernel, tt=tt, nt=nt),
        grid=(batch, nt),
        in_specs=[pl.BlockSpec((tt, 2 * D_A), lambda b, t: (b * nt + t, 0)),
                  _const_spec((CONV_W, D_A)), _const_spec((1, D_A)), _const_spec((1, D_A)),
                  _const_spec((1, D_A))],
        out_specs=[pl.BlockSpec((tt, D_A), lambda b, t: (b * nt + t, 0)),
                   pl.BlockSpec((None, CONV_W - 1, D_A), lambda b, t: (b, 0, 0))],
        out_shape=[jax.ShapeDtypeStruct((n, D_A), F32),
                   jax.ShapeDtypeStruct((batch, CONV_W - 1, D_A), F32)],
        scratch_shapes=[pltpu.VMEM((tt + CONV_HIST, D_A), F32)],
        compiler_params=_cparams(("parallel", "arbitrary")),
        name="conv_branch_prompt",
    )(za, cw, cb.reshape(1, D_A), lg.reshape(1, D_A), lb.reshape(1, D_A))


def _log_decay(zg, wg, bg):
    x = _dot(zg.astype(BF16), wg) + bg
    return (jnp.minimum(x, 0.0) - jnp.log1p(jnp.exp(-jnp.abs(x)))) * (1.0 / GATE_TAU)


def _head_norm_gate(o, hg, r):
    outs = []
    for h in range(H_B):
        oh = o[:, h * DV_B:(h + 1) * DV_B]
        outs.append(oh * lax.rsqrt(jnp.mean(oh * oh, axis=-1, keepdims=True) + EPS) * hg)
    return jnp.concatenate(outs, axis=1) * (r * jax.nn.sigmoid(r))


def _gla_kernel(zqkv_ref, zr_ref, zg_ref, wg_ref, bg_ref, hg_ref, ltri_ref, ones_ref,
                bout_ref, sfin_ref, s_ref, *, tt, nt):
    t = pl.program_id(1)

    @pl.when(t == 0)
    def _():
        s_ref[...] = jnp.zeros((D_BK, D_BV), F32)

    zq = zqkv_ref[...]
    q = zq[:, 0:D_BK] * (DK_B ** -0.5)
    k = zq[:, D_BK:2 * D_BK]
    v = zq[:, 2 * D_BK:]
    la = _log_decay(zg_ref[...], wg_ref[...], bg_ref[...])
    la3 = _split3(la)
    ltri = ltri_ref[...]
    b = _dot(ltri, la3[0]) + _dot(ltri, la3[1]) + _dot(ltri, la3[2])
    ones = ones_ref[...]

    lane_head = lax.broadcasted_iota(I32, (GLA_SUB, D_BK), 1) // DK_B
    bd_mask = (lax.broadcasted_iota(I32, (D_BK, D_BV), 0) // DK_B
               == lax.broadcasted_iota(I32, (D_BK, D_BV), 1) // DV_B)
    o_chunks = []
    for c in range(tt // GLA_CHUNK):
        r0 = c * GLA_CHUNK
        bc = b[r0:r0 + GLA_CHUNK]
        qc = q[r0:r0 + GLA_CHUNK]
        kc = k[r0:r0 + GLA_CHUNK]
        vc = v[r0:r0 + GLA_CHUNK]
        vcb = vc.astype(BF16)
        s_prev = s_ref[...]
        o = _dot((qc * jnp.exp(bc)).astype(BF16), s_prev.astype(BF16))
        o_rows = []
        for i in range(GLA_CHUNK // GLA_SUB):
            i0 = i * GLA_SUB
            nk = i0 + GLA_SUB
            ref_b = bc[i0:i0 + 1, :]
            qi = qc[i0:nk] * jnp.exp(bc[i0:nk] - ref_b)
            ki = kc[0:nk] * jnp.exp(ref_b - bc[0:nk])
            qs = jnp.concatenate([jnp.where(lane_head == h, qi, 0.0) for h in range(H_B)], axis=0)
            att = _dot_nt(qs.astype(BF16), ki.astype(BF16))
            row_i = lax.broadcasted_iota(I32, (H_B * GLA_SUB, nk), 0) % GLA_SUB
            col_j = lax.broadcasted_iota(I32, (H_B * GLA_SUB, nk), 1)
            att = jnp.where(col_j <= row_i + i0, att, 0.0)
            res = _dot(att.astype(BF16), vcb[0:nk])
            o_rows.append(jnp.concatenate(
                [res[h * GLA_SUB:(h + 1) * GLA_SUB, h * DV_B:(h + 1) * DV_B] for h in range(H_B)], axis=1))
        o_chunks.append(o + jnp.concatenate(o_rows, axis=0))
        b_end = bc[GLA_CHUNK - 1:GLA_CHUNK, :]
        kst = kc * jnp.exp(b_end - bc)
        kv = _dot_tn(kst.astype(BF16), vcb)
        b_end_t = (_dot_tn(la3[0][r0:r0 + GLA_CHUNK], ones) + _dot_tn(la3[1][r0:r0 + GLA_CHUNK], ones)
                   + _dot_tn(la3[2][r0:r0 + GLA_CHUNK], ones))
        dec = jnp.exp(b_end_t)
        s_ref[...] = s_prev * jnp.concatenate([dec] * H_B, axis=1) + jnp.where(bd_mask, kv, 0.0)

    o_all = jnp.concatenate(o_chunks, axis=0)
    bout_ref[...] = _head_norm_gate(o_all, hg_ref[...], zr_ref[...])

    @pl.when(t == nt - 1)
    def _():
        for h in range(H_B):
            sfin_ref[h] = s_ref[h * DK_B:(h + 1) * DK_B, h * DV_B:(h + 1) * DV_B]


def gla_prompt(zqkv, zr, zg, wg, bg, hg, batch):
    n = zqkv.shape[0]
    seq = n // batch
    tt = 256
    nt = seq // tt
    ri = lax.broadcasted_iota(I32, (tt, tt), 0)
    ci = lax.broadcasted_iota(I32, (tt, tt), 1)
    ltri = ((ri // GLA_CHUNK == ci // GLA_CHUNK) & (ci <= ri)).astype(BF16)
    ones = jnp.ones((GLA_CHUNK, LANES), BF16)
    row = lambda b, t: (b * nt + t, 0)
    return pl.pallas_call(
        functools.partial(_gla_kernel, tt=tt, nt=nt),
        grid=(batch, nt),
        in_specs=[pl.BlockSpec((tt, 2 * D_BK + D_BV), row), pl.BlockSpec((tt, D_BV), row),
                  pl.BlockSpec((tt, LANES), row), _const_spec((LANES, D_BK)), _const_spec((1, D_BK)),
                  _const_spec((1, DV_B)), _const_spec((tt, tt)), _const_spec((GLA_CHUNK, LANES))],
        out_specs=[pl.BlockSpec((tt, D_BV), row),
                   pl.BlockSpec((None, H_B, DK_B, DV_B), lambda b, t: (b, 0, 0, 0))],
        out_shape=[jax.ShapeDtypeStruct((n, D_BV), F32),
                   jax.ShapeDtypeStruct((batch, H_B, DK_B, DV_B), F32)],
        scratch_shapes=[pltpu.VMEM((D_BK, D_BV), F32)],
        compiler_params=_cparams(("parallel", "arbitrary")),
        name="gla_prompt",
    )(zqkv, zr, zg, wg, bg.reshape(1, D_BK), hg.reshape(1, DV_B), ltri, ones)


def _even_sample_a_kernel(za_ref, zqkv_ref, zg_ref, st_ref, cw_ref, cb_ref, lg_ref, lb_ref, wg_ref, bg_ref,
                          u_ref, aout_ref, q_ref, k_ref, a_ref):
    za = za_ref[...]
    u = za[:, :D_A] * jax.nn.sigmoid(za[:, D_A:])
    u_ref[...] = u
    acc = jnp.zeros_like(u) + cb_ref[...]
    for w in range(CONV_W - 1):
        acc = acc + st_ref[w] * cw_ref[w:w + 1, :]
    acc = acc + u * cw_ref[CONV_W - 1:CONV_W, :]
    aout_ref[...] = _ln_swish(acc, lg_ref[...], lb_ref[...])
    zq = zqkv_ref[...]
    q_ref[...] = zq[:, 0:D_BK] * (DK_B ** -0.5)
    k_ref[...] = zq[:, D_BK:2 * D_BK]
    a_ref[...] = jnp.exp(_log_decay(zg_ref[...], wg_ref[...], bg_ref[...]))


def even_sample_a(za, zqkv, zg, st_t, cw, cb, lg, lb, wg, bg):
    n = za.shape[0]
    shp = lambda w: jax.ShapeDtypeStruct((n, w), F32)
    return pl.pallas_call(
        _even_sample_a_kernel,
        out_shape=[shp(D_A), shp(D_A), shp(D_BK), shp(D_BK), shp(D_BK)],
        compiler_params=pltpu.CompilerParams(vmem_limit_bytes=VMEM_LIMIT),
        name="even_sample_a",
    )(za, zqkv, zg, st_t, cw, cb.reshape(1, D_A), lg.reshape(1, D_A), lb.reshape(1, D_A), wg,
      bg.reshape(1, D_BK))


def _even_sample_b_kernel(a_ref, k_ref, q_ref, v_ref, r_ref, s_ref, hg_ref, snew_ref, bout_ref):
    v = v_ref[...]
    outs = []
    for h in range(H_B):
        rows = slice(h * DK_B, (h + 1) * DK_B)
        vh = v[:, h * DV_B:(h + 1) * DV_B]
        s_new = a_ref[rows, :] * s_ref[h] + k_ref[rows, :] * vh
        snew_ref[h] = s_new
        outs.append(jnp.sum(q_ref[rows, :] * s_new, axis=0, keepdims=True))
    bout_ref[...] = _head_norm_gate(jnp.concatenate(outs, axis=1), hg_ref[...], r_ref[...])


def even_sample_b(a3, k3, q3, v3, r3, state, hg):
    nb = a3.shape[0]
    col = pl.BlockSpec((None, D_BK, 1), lambda b: (b, 0, 0))
    rowv = pl.BlockSpec((None, 1, D_BV), lambda b: (b, 0, 0))
    st = pl.BlockSpec((None, H_B, DK_B, DV_B), lambda b: (b, 0, 0, 0))
    return pl.pallas_call(
        _even_sample_b_kernel,
        grid=(nb,),
        in_specs=[col, col, col, rowv, rowv, st, _const_spec((1, DV_B))],
        out_specs=[st, rowv],
        out_shape=[jax.ShapeDtypeStruct((nb, H_B, DK_B, DV_B), F32),
                   jax.ShapeDtypeStruct((nb, 1, D_BV), F32)],
        compiler_params=_cparams(("parallel",)),
        name="even_sample_b",
    )(a3, k3, q3, v3, r3, state, hg.reshape(1, DV_B))


def _even_out_kernel(x_ref, a_ref, b_ref, w_ref, o_ref):
    y = _dot(a_ref[...].astype(BF16), w_ref[0:D_A, :]) + _dot(b_ref[...].astype(BF16), w_ref[D_A:, :])
    o_ref[...] = x_ref[...] + y


def even_out(x, a, b, w):
    n = x.shape[0]
    tm = min(512, n)
    row = lambda i: (i, 0)
    return pl.pallas_call(
        _even_out_kernel,
        grid=(n // tm,),
        in_specs=[pl.BlockSpec((tm, D_MODEL), row), pl.BlockSpec((tm, D_A), row),
                  pl.BlockSpec((tm, D_BV), row), _const_spec(w.shape)],
        out_specs=pl.BlockSpec((tm, D_MODEL), row),
        out_shape=jax.ShapeDtypeStruct((n, D_MODEL), F32),
        compiler_params=_cparams(("parallel",)),
        name="even_out",
    )(x, a, b, w)


def _mla_prep_kernel(cq_ref, ckv_ref, kpe_ref, kper_ref, cs_ref, sn_ref, qg_ref, kvg_ref,
                     wn_ref, wa_ref, wb_ref, wuk_ref, qc_ref, kc_ref, ckvo_ref, kpeo_ref):
    cs = cs_ref[...]
    sn = sn_ref[...]
    ckv = _rms(ckv_ref[...], kvg_ref[...])
    kpe = kpe_ref[...] * cs + kper_ref[...] * sn
    ckvo_ref[...] = ckv
    kpeo_ref[...] = kpe
    kc_ref[:, 0:R_KV] = ckv.astype(BF16)
    kc_ref[:, R_KV:] = kpe.astype(BF16)
    cqn = _rms(cq_ref[...], qg_ref[...]).astype(BF16)
    qn = _dot(cqn, wn_ref[...]).astype(BF16)
    qa = _dot(cqn, wa_ref[...])
    qb = _dot(cqn, wb_ref[...])
    for h in range(H_C):
        pair = qn[:, (h // 2) * LANES:(h // 2 + 1) * LANES]
        ql = _dot(pair, wuk_ref[h]) * MLA_SCALE
        qp = (qa[:, h * LANES:(h + 1) * LANES] * cs + qb[:, h * LANES:(h + 1) * LANES] * sn) * MLA_SCALE
        qc_ref[h, :, 0:R_KV] = ql.astype(BF16)
        qc_ref[h, :, R_KV:] = qp.astype(BF16)


def mla_prep(cq, ckv, kpe, kper, cs, sn, qg, kvg, wn, wa, wb, wuk, pos_tiles):
    n = cq.shape[0]
    tm = ATT_TQ
    row = lambda i: (i, 0)
    pos = lambda i: (i % pos_tiles, 0)
    return pl.pallas_call(
        _mla_prep_kernel,
        grid=(n // tm,),
        in_specs=[pl.BlockSpec((tm, R_Q), row), pl.BlockSpec((tm, R_KV), row),
                  pl.BlockSpec((tm, LANES), row), pl.BlockSpec((tm, LANES), row),
                  pl.BlockSpec((tm, LANES), pos), pl.BlockSpec((tm, LANES), pos),
                  _const_spec((1, R_Q)), _const_spec((1, R_KV)),
                  _const_spec(wn.shape), _const_spec(wa.shape), _const_spec(wb.shape), _const_spec(wuk.shape)],
        out_specs=[pl.BlockSpec((None, H_C, tm, KC_W), lambda i: (i, 0, 0, 0)),
                   pl.BlockSpec((tm, KC_W), row), pl.BlockSpec((tm, R_KV), row),
                   pl.BlockSpec((tm, LANES), row)],
        out_shape=[jax.ShapeDtypeStruct((n // tm, H_C, tm, KC_W), BF16),
                   jax.ShapeDtypeStruct((n, KC_W), BF16),
                   jax.ShapeDtypeStruct((n, R_KV), F32),
                   jax.ShapeDtypeStruct((n, LANES), F32)],
        compiler_params=_cparams(("parallel",)),
        name="mla_prep",
    )(cq, ckv, kpe, kper, cs, sn, qg.reshape(1, R_Q), kvg.reshape(1, R_KV), wn, wa, wb, wuk)


ATT_STRIP = H_C * ATT_TQ


def _attn_strip_update(q, kc, vt, visible, carry, acc_ref):
    m_old, l_old = carry
    st = _dot_nt(kc, q)
    if visible is not None:
        st = jnp.where(visible, st, NEG_INF)
    m_new = jnp.maximum(m_old, jnp.max(st, axis=0, keepdims=True))
    p = jnp.exp(st - m_new)
    alpha = jnp.exp(m_old - m_new)
    acc_ref[...] = alpha * acc_ref[...] + _dot(vt, p.astype(BF16))
    return m_new, alpha * l_old + jnp.sum(p, axis=0, keepdims=True)


def _mla_attn_kernel(qc_ref, kc_ref, vt_ref, ol_ref, acc_ref):
    i = pl.program_id(1)
    n_full = (i * ATT_TQ) // ATT_TK
    k0 = pl.multiple_of(n_full * ATT_TK, ATT_TK)
    tok = i * ATT_TQ + lax.broadcasted_iota(I32, (ATT_TK, ATT_STRIP), 1) % ATT_TQ
    key = k0 + lax.broadcasted_iota(I32, (ATT_TK, ATT_STRIP), 0)
    causal = key <= tok
    heads = ATT_STRIP // ATT_TQ
    n_strip = H_C // heads
    qs = [qc_ref[heads * s:heads * (s + 1)].reshape(ATT_STRIP, KC_W) for s in range(n_strip)]
    acc_ref[...] = jnp.zeros((n_strip, R_KV, ATT_STRIP), F32)

    def full_block(j, carry):
        kc = kc_ref[pl.ds(pl.multiple_of(j * ATT_TK, ATT_TK), ATT_TK), :]
        vt = vt_ref[j]
        return tuple(_attn_strip_update(qs[s], kc, vt, None, carry[s], acc_ref.at[s]) for s in range(n_strip))

    init = (jnp.full((1, ATT_STRIP), NEG_INF, F32), jnp.zeros((1, ATT_STRIP), F32))
    carry = lax.fori_loop(0, n_full, full_block, (init,) * n_strip)
    kc = kc_ref[pl.ds(k0, ATT_TK), :]
    vt = vt_ref[n_full]
    for s in range(n_strip):
        _, l_fin = _attn_strip_update(qs[s], kc, vt, causal, carry[s], acc_ref.at[s])
        o = (acc_ref[s] / l_fin).T
        ol_ref[heads * s:heads * (s + 1)] = o.reshape(heads, ATT_TQ, R_KV).astype(BF16)


def mla_attn_prompt(qc, kc, batch):
    ntile = qc.shape[0]
    nq = ntile // batch
    seq = nq * ATT_TQ
    nk = seq // ATT_TK
    vt = jnp.transpose(kc[:, :R_KV].reshape(batch, nk, ATT_TK, R_KV), (0, 1, 3, 2))
    return pl.pallas_call(
        _mla_attn_kernel,
        grid=(batch, nq),
        in_specs=[pl.BlockSpec((None, H_C, ATT_TQ, KC_W), lambda b, i: (b * nq + i, 0, 0, 0)),
                  pl.BlockSpec((seq, KC_W), lambda b, i: (b, 0)),
                  pl.BlockSpec((None, nk, R_KV, ATT_TK), lambda b, i: (b, 0, 0, 0))],
        out_specs=pl.BlockSpec((None, H_C, ATT_TQ, R_KV), lambda b, i: (b * nq + i, 0, 0, 0)),
        out_shape=jax.ShapeDtypeStruct((ntile, H_C, ATT_TQ, R_KV), BF16),
        scratch_shapes=[pltpu.VMEM((H_C * ATT_TQ // ATT_STRIP, R_KV, ATT_STRIP), F32)],
        compiler_params=_cparams(("parallel", "arbitrary")),
        name="mla_attn_prompt",
    )(qc, kc, vt)


def _mla_decode_kernel(pt_ref, q_ref, kcur_ref, *refs, ng):
    ckv_refs = refs[:DEC_PAGES]
    kpe_refs = refs[DEC_PAGES:2 * DEC_PAGES]
    o_ref, m_ref, l_ref, acc_ref = refs[2 * DEC_PAGES:]
    g = pl.program_id(1)

    @pl.when(g == 0)
    def _():
        m_ref[...] = jnp.full((H_C, 1), NEG_INF, F32)
        l_ref[...] = jnp.zeros((H_C, 1), F32)
        acc_ref[...] = jnp.zeros((H_C, R_KV), F32)

    q = q_ref[...]
    ql = q[:, 0:R_KV]
    qp = q[:, R_KV:R_KV + D_ROPE]
    cks = [r[...].astype(BF16) for r in ckv_refs]
    s = jnp.concatenate(
        [_dot_nt(ql, cks[p]) + _dot_nt(qp, kpe_refs[p][...].astype(BF16)) for p in range(DEC_PAGES)], axis=1)
    m_old = m_ref[...]
    m_new = jnp.maximum(m_old, jnp.max(s, axis=1, keepdims=True))
    p_all = jnp.exp(s - m_new).astype(BF16)
    alpha = jnp.exp(m_old - m_new)
    pv = _dot(p_all[:, 0:PAGE_SIZE], cks[0])
    for p in range(1, DEC_PAGES):
        pv = pv + _dot(p_all[:, p * PAGE_SIZE:(p + 1) * PAGE_SIZE], cks[p])
    l_new = alpha * l_ref[...] + jnp.sum(p_all.astype(F32), axis=1, keepdims=True)
    acc_new = alpha * acc_ref[...] + pv
    m_ref[...] = m_new
    l_ref[...] = l_new
    acc_ref[...] = acc_new

    @pl.when(g == ng - 1)
    def _():
        kcur = kcur_ref[...].astype(F32)
        s_cur = jnp.sum(q.astype(F32) * kcur, axis=1, keepdims=True)
        m_fin = jnp.maximum(m_new, s_cur)
        a_fin = jnp.exp(m_new - m_fin)
        p_cur = jnp.exp(s_cur - m_fin)
        l_fin = a_fin * l_new + p_cur
        o_ref[...] = (a_fin * acc_new + p_cur * kcur[:, 0:R_KV]) / l_fin


def mla_decode(page_table, q_s, kc_s, cache_ckv, cache_kpe, layer):
    nb, npages = page_table.shape
    ng = npages // DEC_PAGES

    def page_spec(width, p):
        return pl.BlockSpec((None, None, PAGE_SIZE, width),
                            lambda b, g, pt: (layer, pt[b, g * DEC_PAGES + p], 0, 0))

    grid_spec = pltpu.PrefetchScalarGridSpec(
        num_scalar_prefetch=1,
        grid=(nb, ng),
        in_specs=([pl.BlockSpec((None, H_C, KC_W), lambda b, g, pt: (b, 0, 0)),
                   pl.BlockSpec((None, 1, KC_W), lambda b, g, pt: (b, 0, 0))]
                  + [page_spec(R_KV, p) for p in range(DEC_PAGES)]
                  + [page_spec(D_ROPE, p) for p in range(DEC_PAGES)]),
        out_specs=pl.BlockSpec((None, H_C, R_KV), lambda b, g, pt: (b, 0, 0)),
        scratch_shapes=[pltpu.VMEM((H_C, 1), F32), pltpu.VMEM((H_C, 1), F32), pltpu.VMEM((H_C, R_KV), F32)],
    )
    return pl.pallas_call(
        functools.partial(_mla_decode_kernel, ng=ng),
        grid_spec=grid_spec,
        out_shape=jax.ShapeDtypeStruct((nb, H_C, R_KV), F32),
        compiler_params=_cparams(("parallel", "arbitrary")),
        name="mla_decode",
    )(page_table, q_s, kc_s, *([cache_ckv] * DEC_PAGES), *([cache_kpe] * DEC_PAGES))


def _mla_out_kernel(x_ref, ol_ref, wuv_ref, wo_ref, o_ref):
    pairs = []
    for p in range(H_C // 2):
        pairs.append((_dot(ol_ref[2 * p], wuv_ref[2 * p]) + _dot(ol_ref[2 * p + 1], wuv_ref[2 * p + 1])).astype(BF16))
    o = jnp.concatenate(pairs, axis=1)
    o_ref[...] = x_ref[...] + _dot(o, wo_ref[...])


def mla_out(x, ol, wuv, wo):
    n = x.shape[0]
    tm = ATT_TQ
    return pl.pallas_call(
        _mla_out_kernel,
        grid=(n // tm,),
        in_specs=[pl.BlockSpec((tm, D_MODEL), lambda i: (i, 0)),
                  pl.BlockSpec((None, H_C, tm, R_KV), lambda i: (i, 0, 0, 0)),
                  _const_spec(wuv.shape), _const_spec(wo.shape)],
        out_specs=pl.BlockSpec((tm, D_MODEL), lambda i: (i, 0)),
        out_shape=jax.ShapeDtypeStruct((n, D_MODEL), F32),
        compiler_params=_cparams(("parallel",)),
        name="mla_out",
    )(x, ol, wuv, wo)


def _top16_rows(s):
    rows, tm = s.shape
    iota = lax.broadcasted_iota(I32, (rows, tm), 0)
    rank = jnp.full((rows, tm), TOPK, I32)
    vals = []
    cur = s
    for it in range(TOPK):
        m = jnp.max(cur, axis=0, keepdims=True)
        idx = jnp.min(jnp.where(cur == m, iota, rows), axis=0, keepdims=True)
        sel = iota == idx
        rank = jnp.where(sel, it, rank)
        cur = jnp.where(sel, NEG_INF, cur)
        vals.append(m)
    return jnp.concatenate(vals, axis=0), rank


CAND_ROWS = 80


def _pair_top16(v1, v2):
    cand, _, flat = _pair_candidates(v1, v2)
    sums, a_sel = [], []
    for _ in range(TOPK):
        m = jnp.max(cand, axis=0, keepdims=True)
        ci = jnp.min(jnp.where(cand == m, flat, TOPK * TOPK), axis=0, keepdims=True)
        cand = jnp.where(flat == ci, NEG_INF, cand)
        sums.append(m)
        a_sel.append(ci >> 4)
    return sums, a_sel


def _top16_rows_distinct(s):
    rows, tm = s.shape
    rank = jnp.full((rows, tm), float(TOPK), F32)
    vals = []
    cur = s
    for it in range(TOPK):
        m = jnp.max(cur, axis=0, keepdims=True)
        sel = cur == m
        rank = jnp.where(sel, float(it), rank)
        cur = jnp.where(sel, NEG_INF, cur)
        vals.append(m)
    marked = jnp.sum(jnp.where(rank < float(TOPK), 1.0, 0.0), axis=0, keepdims=True)
    return jnp.concatenate(vals, axis=0), rank, marked


def _pair_candidates(v1, v2):
    tm = v1.shape[1]
    pieces = [v1[0:1] + v2]
    for a in range(1, 8):
        pieces.append(v1[a:a + 1] + v2[0:8])
    pieces.append(v1[8:16] + v2[0:1])
    cand = jnp.concatenate(pieces, axis=0)
    r = lax.broadcasted_iota(I32, (CAND_ROWS, tm), 0)
    a_idx = jnp.where(r < 16, 0, jnp.where(r < 72, 1 + ((r - 16) >> 3), r - 64))
    b_idx = jnp.where(r < 16, r, jnp.where(r < 72, (r - 16) & 7, 0))
    cand = jnp.where((a_idx + 1) * (b_idx + 1) <= TOPK, cand, NEG_INF)
    return cand, a_idx, a_idx * TOPK + b_idx


def _pair_top16_distinct(v1, v2):
    cand, _, _ = _pair_candidates(v1, v2)
    cur = cand
    sums = []
    for _ in range(TOPK):
        m = jnp.max(cur, axis=0, keepdims=True)
        cur = jnp.where(cur == m, NEG_INF, cur)
        sums.append(m)
    picked = jnp.where((cur == NEG_INF) & (cand > NEG_INF), 1.0, 0.0)
    counts = [jnp.sum(picked[0:16], axis=0, keepdims=True)]
    for a in range(1, 8):
        counts.append(jnp.sum(picked[8 + 8 * a:16 + 8 * a], axis=0, keepdims=True))
    cnt = jnp.concatenate(counts + [picked[72:80]], axis=0)
    return sums, cnt, jnp.sum(picked, axis=0, keepdims=True)


def _route_tables(s1, s2, rank1, rank2, v1, v2, sums, cnt, refs, h):
    r2_ref, bt_ref, e1_ref, e2_ref = refs
    z = jnp.exp(sums[0] - sums[0])
    for kk in range(1, TOPK):
        z = z + jnp.exp(sums[kk] - sums[0])
    bt = jnp.zeros(s1.shape, F32)
    for a in range(TOPK):
        bt = jnp.where(rank1 == a, cnt[a:a + 1], bt)
    e1 = jnp.exp(s1 - v1[0:1]) * (1.0 / z)
    e2 = jnp.exp(s2 - v2[0:1])
    r2 = rank2.astype(F32)
    for c in range(s1.shape[1] // LANES):
        cols = slice(c * LANES, (c + 1) * LANES)
        bt_ref[h, c] = bt[:, cols]
        e1_ref[h, c] = e1[:, cols]
        e2_ref[h, c] = e2[:, cols].astype(BF16)
        r2_ref[h, c] = r2[:, cols].astype(BF16)


def _peer_route_kernel(x_ref, g_ref, wq_ref, keys_ref, xn_ref, r2_ref, bt_ref, e1_ref, e2_ref, q_ref, bad_ref):
    xn = _rms(x_ref[...], g_ref[...]).astype(BF16)
    xn_ref[...] = xn
    q = _dot(xn, wq_ref[...]).astype(BF16)
    tm = q.shape[0]
    for k in range(2 * H_P):
        q_ref[k] = q[:, k * D_HALF:(k + 1) * D_HALF]
    refs = (r2_ref, bt_ref, e1_ref, e2_ref)

    def scores(h):
        return _dot_nt(keys_ref[h, 0], q_ref[2 * h]), _dot_nt(keys_ref[h, 1], q_ref[2 * h + 1])

    for h in range(H_P):
        s1, s2 = scores(h)
        v1, rank1, n1 = _top16_rows_distinct(s1)
        v2, rank2, n2 = _top16_rows_distinct(s2)
        sums, cnt, n3 = _pair_top16_distinct(v1, v2)
        _route_tables(s1, s2, rank1, rank2, v1, v2, sums, cnt, refs, h)
        bad_ref[h:h + 1, :] = jnp.abs(n1 - TOPK) + jnp.abs(n2 - TOPK) + jnp.abs(n3 - TOPK)

    def redo(h, carry):
        @pl.when(jnp.max(bad_ref[pl.ds(h, 1), :]) > 0.0)
        def _():
            iota16 = lax.broadcasted_iota(I32, (TOPK, tm), 0)
            s1, s2 = scores(h)
            v1, rank1 = _top16_rows(s1)
            v2, rank2 = _top16_rows(s2)
            sums, a_sel = _pair_top16(v1, v2)
            cnt = jnp.zeros((TOPK, tm), F32)
            for kk in range(TOPK):
                cnt = cnt + jnp.where(iota16 == a_sel[kk], 1.0, 0.0)
            _route_tables(s1, s2, rank1, rank2, v1, v2, sums, cnt, refs, h)
        return carry

    lax.fori_loop(0, H_P, redo, 0)


def peer_route(x, g, wq, keys):
    n = x.shape[0]
    tm = min(256, n)
    tab = pl.BlockSpec((H_P, tm // LANES, N_KEYS, LANES), lambda i: (0, i, 0, 0))
    tab_shape = lambda dt: jax.ShapeDtypeStruct((H_P, n // LANES, N_KEYS, LANES), dt)
    return pl.pallas_call(
        _peer_route_kernel,
        grid=(n // tm,),
        in_specs=[pl.BlockSpec((tm, D_MODEL), lambda i: (i, 0)), _const_spec((1, D_MODEL)),
                  _const_spec(wq.shape), _const_spec(keys.shape)],
        out_specs=[pl.BlockSpec((tm, D_MODEL), lambda i: (i, 0)), tab, tab, tab, tab],
        out_shape=[jax.ShapeDtypeStruct((n, D_MODEL), BF16), tab_shape(BF16), tab_shape(F32), tab_shape(F32),
                   tab_shape(BF16)],
        scratch_shapes=[pltpu.VMEM((2 * H_P, tm, D_HALF), BF16), pltpu.VMEM((H_P, tm), F32)],
        compiler_params=_cparams(("parallel",)),
        name="peer_route",
    )(x, g.reshape(1, D_MODEL), wq, keys)


def _peer_dense_kernel(x_ref, xn_ref, r2_ref, bt_ref, e1_ref, e2_ref, u_ref, vt_ref, o_ref,
                       acc_ref, ht_ref, act_ref, *, tm, te, ne):
    e = pl.program_id(1)

    @pl.when(e == 0)
    def _():
        acc_ref[...] = jnp.zeros_like(acc_ref)

    assert te == SUBLANES * N_KEYS
    ht_ref[...] = _dot_nt(u_ref[...], xn_ref[...])

    def token_row(ref, h, c, ii):
        return ref[h, c, e, ii:ii + 1, :].astype(BF16)

    for pair in range(SUBLANES // 2):
        iis = (2 * pair, 2 * pair + 1)
        for c in range(tm // LANES):
            cols = slice(c * LANES, (c + 1) * LANES)
            gates = [jnp.zeros((N_KEYS, LANES), BF16) for _ in iis]
            for h in range(H_P):
                r2 = r2_ref[h, c]
                e2 = e2_ref[h, c]
                for k, ii in enumerate(iis):
                    val = e2 * token_row(e1_ref, h, c, ii)
                    gates[k] = gates[k] + jnp.where(r2 < token_row(bt_ref, h, c, ii), val, jnp.zeros_like(val))
            for k, ii in enumerate(iis):
                rows = slice(ii * N_KEYS, (ii + 1) * N_KEYS)
                ht = ht_ref[rows, cols]
                gelu = (0.5 * ht * (1.0 + lax.erf(ht * (2.0 ** -0.5)))).astype(BF16)
                act_ref[rows, cols] = gelu * gates[k]
    acc_ref[...] += _dot(vt_ref[...], act_ref[...])

    @pl.when(e == ne - 1)
    def _():
        o_ref[...] = x_ref[...] + acc_ref[...].T


def peer_dense(x, xn, r2, bt, e1, e2, u, vt):
    n = x.shape[0]
    tm = min(512, n)
    te = SUBLANES * N_KEYS
    ne = N_EXPERTS // te
    f32_view = (H_P, n // LANES, N_KEYS // SUBLANES, SUBLANES, LANES)
    bf16_view = (H_P, n // LANES, N_KEYS, LANES)
    tab = lambda view: pl.BlockSpec((H_P, tm // LANES) + view[2:], lambda i, e: (0, i) + (0,) * (len(view) - 2))
    return pl.pallas_call(
        functools.partial(_peer_dense_kernel, tm=tm, te=te, ne=ne),
        grid=(n // tm, ne),
        in_specs=[pl.BlockSpec((tm, D_MODEL), lambda i, e: (i, 0)),
                  pl.BlockSpec((tm, D_MODEL), lambda i, e: (i, 0)),
                  tab(bf16_view), tab(f32_view), tab(f32_view), tab(bf16_view),
                  pl.BlockSpec((te, D_MODEL), lambda i, e: (e, 0)),
                  pl.BlockSpec((D_MODEL, te), lambda i, e: (0, e))],
        out_specs=pl.BlockSpec((tm, D_MODEL), lambda i, e: (i, 0)),
        out_shape=jax.ShapeDtypeStruct((n, D_MODEL), F32),
        scratch_shapes=[pltpu.VMEM((D_MODEL, tm), F32), pltpu.VMEM((te, tm), F32),
                        pltpu.VMEM((te, tm), BF16)],
        compiler_params=_cparams(("parallel", "arbitrary")),
        name="peer_dense",
    )(x, xn, r2.reshape(bf16_view), bt.reshape(f32_view), e1.reshape(f32_view), e2.reshape(bf16_view), u, vt)


def peer(x, g, wq, keys, u, vt):
    xn, r2, bt, e1, e2 = peer_route(x, g, wq, keys)
    return peer_dense(x, xn, r2, bt, e1, e2, u, vt)


def _pad_cols(w, width):
    return jnp.pad(w, ((0, 0), (0, width - w.shape[1])))


def _rope_tables(pos):
    half = D_ROPE // 2
    inv = jnp.power(ROPE_BASE, -jnp.arange(half, dtype=F32) / half)
    ang = pos.astype(F32)[:, None] * inv[None, :]
    cos2 = jnp.concatenate([jnp.cos(ang), jnp.cos(ang)], axis=1)
    sin2 = jnp.concatenate([jnp.sin(ang), jnp.sin(ang)], axis=1)
    return _pad_cols(cos2, LANES), _pad_cols(sin2, LANES)


def _rot_half_cols(w):
    half = D_ROPE // 2
    return jnp.concatenate([-w[..., half:], w[..., :half]], axis=-1)


def _even_layer_weights(w_in, w_gate, w_out):
    w_in_p = _pad_cols(w_in, 2 * D_A + 2 * D_BK + 2 * D_BV + LANES).astype(BF16)
    wg = jnp.pad(w_gate, ((0, LANES - GATE_RANK), (0, 0))).astype(BF16)
    return w_in_p, wg, w_out.astype(BF16)


def _odd_layer_weights(w_in, w_q_b, w_uk, w_uv, w_out):
    wk = w_in[:, R_Q + R_KV:]
    w_in_p = jnp.concatenate([w_in[:, :R_Q + R_KV], _pad_cols(wk, LANES), _pad_cols(_rot_half_cols(wk), LANES)],
                             axis=1).astype(BF16)
    wq3 = w_q_b.reshape(R_Q, H_C, D_NOPE + D_ROPE)
    wn = wq3[:, :, :D_NOPE].reshape(R_Q, H_C * D_NOPE).astype(BF16)
    wpe = wq3[:, :, D_NOPE:]
    pad3 = lambda w: jnp.pad(w, ((0, 0), (0, 0), (0, LANES - D_ROPE))).reshape(R_Q, H_C * LANES).astype(BF16)
    wa, wb = pad3(wpe), pad3(_rot_half_cols(wpe))
    ukt = jnp.transpose(w_uk, (1, 2, 0))
    uv = jnp.transpose(w_uv, (1, 0, 2))
    odd = (jnp.arange(H_C) % 2 == 1)[:, None, None]
    zk = jnp.zeros_like(ukt)
    wuk = jnp.where(odd, jnp.concatenate([zk, ukt], axis=1), jnp.concatenate([ukt, zk], axis=1)).astype(BF16)
    zv = jnp.zeros_like(uv)
    wuv = jnp.where(odd, jnp.concatenate([zv, uv], axis=2), jnp.concatenate([uv, zv], axis=2)).astype(BF16)
    return w_in_p, wn, wa, wb, wuk, wuv, w_out.astype(BF16)


def kernel(x_prompt, x_sample, cache_ckv, cache_kpe, state_conv, state_gla, page_table, ab_norm, ab_w_in,
           conv_w, conv_b, conv_ln_g, conv_ln_b, gla_w_gate, gla_b_gate, gla_head_g, ab_w_out, c_norm, c_w_in,
           c_q_norm, c_kv_norm, c_w_q_b, c_w_uk, c_w_uv, c_w_out, ffn_norm, peer_w_q, peer_keys, peer_u,
           peer_v, final_norm):
    bp, tp, _ = x_prompt.shape
    bs = x_sample.shape[0]
    n_p = bp * tp
    xp = x_prompt.reshape(n_p, D_MODEL)
    xs = jnp.pad(x_sample.reshape(bs, D_MODEL), ((0, SAMPLE_ROWS - bs), (0, 0)))
    cs_p, sn_p = _rope_tables(jnp.arange(tp, dtype=I32))
    cs_s, sn_s = _rope_tables(jnp.full((SAMPLE_ROWS,), PAST_LEN, I32))
    ckv_p, kpe_p, ckv_s, kpe_s = [], [], [], []
    conv_p, conv_s, gla_p, gla_s = [], [], [], []
    for l in range(DEPTH):
        j = l // 2
        if l % 2 == 0:
            w_in, wg, w_out = _even_layer_weights(ab_w_in[j], gla_w_gate[j], ab_w_out[j])
            widths = (2 * D_A, 2 * D_BK + D_BV, D_BV, LANES)
            za, zqkv, zr, zg = norm_matmul(xp, ab_norm[j], w_in, widths)
            a_out, cbuf = conv_branch_prompt(za, conv_w[j], conv_b[j], conv_ln_g[j], conv_ln_b[j], bp)
            b_out, s_fin = gla_prompt(zqkv, zr, zg, wg, gla_b_gate[j], gla_head_g[j], bp)
            xp = even_out(xp, a_out, b_out, w_out)
            conv_p.append(cbuf)
            gla_p.append(s_fin)
            za, zqkv, zr, zg = norm_matmul(xs, ab_norm[j], w_in, widths)
            st_t = jnp.pad(jnp.transpose(state_conv[j], (1, 0, 2)), ((0, 0), (0, SAMPLE_ROWS - bs), (0, 0)))
            u, a_out, qg, kg, ag = even_sample_a(za, zqkv, zg, st_t, conv_w[j], conv_b[j], conv_ln_g[j],
                                                 conv_ln_b[j], wg, gla_b_gate[j])
            col = lambda t: t[:bs].reshape(bs, D_BK, 1)
            s_new, b_out = even_sample_b(col(ag), col(kg), col(qg),
                                         zqkv[:bs, 2 * D_BK:].reshape(bs, 1, D_BV),
                                         zr[:bs].reshape(bs, 1, D_BV), state_gla[j], gla_head_g[j])
            b_out = jnp.pad(b_out.reshape(bs, D_BV), ((0, SAMPLE_ROWS - bs), (0, 0)))
            xs = even_out(xs, a_out, b_out, w_out)
            conv_s.append(jnp.concatenate([state_conv[j][:, 1:], u[:bs, None, :]], axis=1))
            gla_s.append(s_new)
        else:
            w_in, wn, wa, wb, wuk, wuv, w_out = _odd_layer_weights(c_w_in[j], c_w_q_b[j], c_w_uk[j], c_w_uv[j],
                                                                   c_w_out[j])
            widths = (R_Q, R_KV, LANES, LANES)
            cq, ckv, kpe, kper = norm_matmul(xp, c_norm[j], w_in, widths)
            qc, kc, ckv_o, kpe_o = mla_prep(cq, ckv, kpe, kper, cs_p, sn_p, c_q_norm[j], c_kv_norm[j],
                                            wn, wa, wb, wuk, tp // ATT_TQ)
            ol = mla_attn_prompt(qc, kc, bp)
            xp = mla_out(xp, ol, wuv, w_out)
            ckv_p.append(ckv_o.reshape(bp, tp, R_KV))
            kpe_p.append(kpe_o[:, :D_ROPE].reshape(bp, tp, D_ROPE))
            cq, ckv, kpe, kper = norm_matmul(xs, c_norm[j], w_in, widths)
            qc, kc, ckv_o, kpe_o = mla_prep(cq, ckv, kpe, kper, cs_s, sn_s, c_q_norm[j], c_kv_norm[j],
                                            wn, wa, wb, wuk, 1)
            q_s = jnp.transpose(qc[0, :, :bs, :], (1, 0, 2))
            o_lat = mla_decode(page_table, q_s, kc[:bs].reshape(bs, 1, KC_W), cache_ckv, cache_kpe, j)
            ol = jnp.pad(jnp.transpose(o_lat, (1, 0, 2)), ((0, 0), (0, SAMPLE_ROWS - bs), (0, 0)))
            xs = mla_out(xs, ol.astype(BF16)[None], wuv, w_out)
            ckv_s.append(ckv_o[:bs].reshape(bs, 1, R_KV))
            kpe_s.append(kpe_o[:bs, :D_ROPE].reshape(bs, 1, D_ROPE))
        wq = peer_w_q[l].astype(BF16)
        keys = peer_keys[l].astype(BF16)
        u_tab = peer_u[l].astype(BF16)
        vt_tab = peer_v[l].astype(BF16).T
        xp = peer(xp, ffn_norm[l], wq, keys, u_tab, vt_tab)
        xs = peer(xs, ffn_norm[l], wq, keys, u_tab, vt_tab)
    y_prompt = final_rms(xp, final_norm).reshape(bp, tp, D_MODEL)
    y_sample = final_rms(xs, final_norm)[:bs].reshape(bs, 1, D_MODEL)
    return (y_prompt, y_sample,
            jnp.stack(ckv_p), jnp.stack(kpe_p), jnp.stack(ckv_s), jnp.stack(kpe_s),
            jnp.stack(conv_p), jnp.stack(conv_s), jnp.stack(gla_p), jnp.stack(gla_s))
```

```python
import functools

import jax
import jax.numpy as jnp
from jax import lax
from jax.experimental import pallas as pl
from jax.experimental.pallas import tpu as pltpu

F32 = jnp.float32
BF16 = jnp.bfloat16
I32 = jnp.int32

D_MODEL = 1024
SEQ = 2048
DEPTH = 4
DEC_BATCH = 32
PAST_LEN = 16384
PAGE_SIZE = 128
EPS = 1e-6
D_A = 512
CONV_W = 31
H_B = 4
D_BV = 512
D_BK = 256
DK_B = 64
DV_B = 128
GATE_RANK = 16
GATE_TAU = 16.0
GLA_CHUNK = 64
GLA_SUB = 16
H_C = 16
D_NOPE = 64
D_ROPE = 32
D_V = 64
R_Q = 384
R_KV = 256
ROPE_BASE = 10000.0
MLA_SCALE = (D_NOPE + D_ROPE) ** -0.5
H_P = 8
N_KEYS = 128
N_EXPERTS = N_KEYS * N_KEYS
D_HALF = 128
TOPK = 16

LANES = 128
SUBLANES = 8
SAMPLE_ROWS = 128
KC_W = R_KV + LANES
ATT_TQ = 128
ATT_TK = 512
DEC_PAGES = 16
VMEM_LIMIT = 56 * 1024 * 1024
NEG_INF = float("-inf")


def _cparams(sem):
    return pltpu.CompilerParams(dimension_semantics=sem, vmem_limit_bytes=VMEM_LIMIT)


def _dot(a, b):
    return jnp.dot(a, b, preferred_element_type=F32)


def _dot_nt(a, b):
    return lax.dot_general(a, b, (((1,), (1,)), ((), ())), preferred_element_type=F32)


def _dot_tn(a, b):
    return lax.dot_general(a, b, (((0,), (0,)), ((), ())), preferred_element_type=F32)


def _rms(x, g):
    return x * lax.rsqrt(jnp.mean(x * x, axis=-1, keepdims=True) + EPS) * g


def _split3(x):
    hi = x.astype(BF16)
    r1 = x - hi.astype(F32)
    mid = r1.astype(BF16)
    lo = (r1 - mid.astype(F32)).astype(BF16)
    return hi, mid, lo


def _const_spec(shape):
    nd = len(shape)
    return pl.BlockSpec(shape, lambda *_: (0,) * nd)


def _norm_mm_kernel(x_ref, g_ref, w_ref, *out_refs, widths):
    xn = _rms(x_ref[...], g_ref[...])
    y = _dot(xn.astype(BF16), w_ref[...])
    off = 0
    for o_ref, w in zip(out_refs, widths):
        o_ref[...] = y[:, off:off + w]
        off += w


def norm_matmul(x, g, w, widths):
    n, k = x.shape
    tm = min(512, n)
    return pl.pallas_call(
        functools.partial(_norm_mm_kernel, widths=widths),
        grid=(n // tm,),
        in_specs=[pl.BlockSpec((tm, k), lambda i: (i, 0)), _const_spec((1, k)), _const_spec(w.shape)],
        out_specs=[pl.BlockSpec((tm, wd), lambda i: (i, 0)) for wd in widths],
        out_shape=[jax.ShapeDtypeStruct((n, wd), F32) for wd in widths],
        compiler_params=_cparams(("parallel",)),
        name="norm_matmul",
    )(x, g.reshape(1, k), w)


def _final_norm_kernel(x_ref, g_ref, o_ref):
    o_ref[...] = _rms(x_ref[...], g_ref[...])


def final_rms(x, g):
    n, k = x.shape
    tm = min(512, n)
    return pl.pallas_call(
        _final_norm_kernel,
        grid=(n // tm,),
        in_specs=[pl.BlockSpec((tm, k), lambda i: (i, 0)), _const_spec((1, k))],
        out_specs=pl.BlockSpec((tm, k), lambda i: (i, 0)),
        out_shape=jax.ShapeDtypeStruct((n, k), F32),
        compiler_params=_cparams(("parallel",)),
        name="final_norm",
    )(x, g.reshape(1, k))


def _ln_swish(c, g, b):
    mu = jnp.mean(c, axis=-1, keepdims=True)
    xc = c - mu
    y = xc * lax.rsqrt(jnp.mean(xc * xc, axis=-1, keepdims=True) + EPS) * g + b
    return y * jax.nn.sigmoid(y)


CONV_HIST = 32


def _conv_kernel(za_ref, cw_ref, cb_ref, lg_ref, lb_ref, out_ref, buf_ref, ext_ref, *, tt, nt):
    t = pl.program_id(1)

    @pl.when(t == 0)
    def _():
        ext_ref[0:CONV_HIST, :] = jnp.zeros((CONV_HIST, D_A), F32)

    za = za_ref[...]
    ext_ref[CONV_HIST:CONV_HIST + tt, :] = za[:, :D_A] * jax.nn.sigmoid(za[:, D_A:])
    first = CONV_HIST - (CONV_W - 1)
    acc = jnp.zeros((tt, D_A), F32) + cb_ref[...]
    for w in range(CONV_W):
        acc = acc + ext_ref[pl.ds(first + w, tt), :] * cw_ref[w:w + 1, :]
    out_ref[...] = _ln_swish(acc, lg_ref[...], lb_ref[...])

    @pl.when(t == nt - 1)
    def _():
        buf_ref[...] = ext_ref[pl.ds(tt + first, CONV_W - 1), :]

    ext_ref[0:CONV_HIST, :] = ext_ref[tt:tt + CONV_HIST, :]


def conv_branch_prompt(za, cw, cb, lg, lb, batch):
    n = za.shape[0]
    seq = n // batch
    tt = 512
    nt = seq // tt
    return pl.pallas_call(
        functools.partial(_conv_kernel, tt=tt, nt=nt),
        grid=(batch, nt),
        in_specs=[pl.BlockSpec((tt, 2 * D_A), lambda b, t: (b * nt + t, 0)),
                  _const_spec((CONV_W, D_A)), _const_spec((1, D_A)), _const_spec((1, D_A)),
                  _const_spec((1, D_A))],
        out_specs=[pl.BlockSpec((tt, D_A), lambda b, t: (b * nt + t, 0)),
                   pl.BlockSpec((None, CONV_W - 1, D_A), lambda b, t: (b, 0, 0))],
        out_shape=[jax.ShapeDtypeStruct((n, D_A), F32),
                   jax.ShapeDtypeStruct((batch, CONV_W - 1, D_A), F32)],
        scratch_shapes=[pltpu.VMEM((tt + CONV_HIST, D_A), F32)],
        compiler_params=_cparams(("parallel", "arbitrary")),
        name="conv_branch_prompt",
    )(za, cw, cb.reshape(1, D_A), lg.reshape(1, D_A), lb.reshape(1, D_A))


def _log_decay(zg, wg, bg):
    x = _dot(zg.astype(BF16), wg) + bg
    return (jnp.minimum(x, 0.0) - jnp.log1p(jnp.exp(-jnp.abs(x)))) * (1.0 / GATE_TAU)


def _head_norm_gate(o, hg, r):
    outs = []
    for h in range(H_B):
        oh = o[:, h * DV_B:(h + 1) * DV_B]
        outs.append(oh * lax.rsqrt(jnp.mean(oh * oh, axis=-1, keepdims=True) + EPS) * hg)
    return jnp.concatenate(outs, axis=1) * (r * jax.nn.sigmoid(r))


def _gla_kernel(zqkv_ref, zr_ref, zg_ref, wg_ref, bg_ref, hg_ref, ltri_ref, ones_ref,
                bout_ref, sfin_ref, s_ref, *, tt, nt):
    t = pl.program_id(1)

    @pl.when(t == 0)
    def _():
        s_ref[...] = jnp.zeros((D_BK, D_BV), F32)

    zq = zqkv_ref[...]
    q = zq[:, 0:D_BK] * (DK_B ** -0.5)
    k = zq[:, D_BK:2 * D_BK]
    v = zq[:, 2 * D_BK:]
    la = _log_decay(zg_ref[...], wg_ref[...], bg_ref[...])
    la3 = _split3(la)
    ltri = ltri_ref[...]
    b = _dot(ltri, la3[0]) + _dot(ltri, la3[1]) + _dot(ltri, la3[2])
    ones = ones_ref[...]

    lane_head = lax.broadcasted_iota(I32, (GLA_SUB, D_BK), 1) // DK_B
    bd_mask = (lax.broadcasted_iota(I32, (D_BK, D_BV), 0) // DK_B
               == lax.broadcasted_iota(I32, (D_BK, D_BV), 1) // DV_B)
    o_chunks = []
    for c in range(tt // GLA_CHUNK):
        r0 = c * GLA_CHUNK
        bc = b[r0:r0 + GLA_CHUNK]
        qc = q[r0:r0 + GLA_CHUNK]
        kc = k[r0:r0 + GLA_CHUNK]
        vc = v[r0:r0 + GLA_CHUNK]
        vcb = vc.astype(BF16)
        s_prev = s_ref[...]
        o = _dot((qc * jnp.exp(bc)).astype(BF16), s_prev.astype(BF16))
        o_rows = []
        for i in range(GLA_CHUNK // GLA_SUB):
            i0 = i * GLA_SUB
            nk = i0 + GLA_SUB
            ref_b = bc[i0:i0 + 1, :]
            qi = qc[i0:nk] * jnp.exp(bc[i0:nk] - ref_b)
            ki = kc[0:nk] * jnp.exp(ref_b - bc[0:nk])
            qs = jnp.concatenate([jnp.where(lane_head == h, qi, 0.0) for h in range(H_B)], axis=0)
            att = _dot_nt(qs.astype(BF16), ki.astype(BF16))
            row_i = lax.broadcasted_iota(I32, (H_B * GLA_SUB, nk), 0) % GLA_SUB
            col_j = lax.broadcasted_iota(I32, (H_B * GLA_SUB, nk), 1)
            att = jnp.where(col_j <= row_i + i0, att, 0.0)
            res = _dot(att.astype(BF16), vcb[0:nk])
            o_rows.append(jnp.concatenate(
                [res[h * GLA_SUB:(h + 1) * GLA_SUB, h * DV_B:(h + 1) * DV_B] for h in range(H_B)], axis=1))
        o_chunks.append(o + jnp.concatenate(o_rows, axis=0))
        b_end = bc[GLA_CHUNK - 1:GLA_CHUNK, :]
        kst = kc * jnp.exp(b_end - bc)
        kv = _dot_tn(kst.astype(BF16), vcb)
        b_end_t = (_dot_tn(la3[0][r0:r0 + GLA_CHUNK], ones) + _dot_tn(la3[1][r0:r0 + GLA_CHUNK], ones)
                   + _dot_tn(la3[2][r0:r0 + GLA_CHUNK], ones))
        dec = jnp.exp(b_end_t)
        s_ref[...] = s_prev * jnp.concatenate([dec] * H_B, axis=1) + jnp.where(bd_mask, kv, 0.0)

    o_all = jnp.concatenate(o_chunks, axis=0)
    bout_ref[...] = _head_norm_gate(o_all, hg_ref[...], zr_ref[...])

    @pl.when(t == nt - 1)
    def _():
        for h in range(H_B):
            sfin_ref[h] = s_ref[h * DK_B:(h + 1) * DK_B, h * DV_B:(h + 1) * DV_B]


def gla_prompt(zqkv, zr, zg, wg, bg, hg, batch):
    n = zqkv.shape[0]
    seq = n // batch
    tt = 256
    nt = seq // tt
    ri = lax.broadcasted_iota(I32, (tt, tt), 0)
    ci = lax.broadcasted_iota(I32, (tt, tt), 1)
    ltri = ((ri // GLA_CHUNK == ci // GLA_CHUNK) & (ci <= ri)).astype(BF16)
    ones = jnp.ones((GLA_CHUNK, LANES), BF16)
    row = lambda b, t: (b * nt + t, 0)
    return pl.pallas_call(
        functools.partial(_gla_kernel, tt=tt, nt=nt),
        grid=(batch, nt),
        in_specs=[pl.BlockSpec((tt, 2 * D_BK + D_BV), row), pl.BlockSpec((tt, D_BV), row),
                  pl.BlockSpec((tt, LANES), row), _const_spec((LANES, D_BK)), _const_spec((1, D_BK)),
                  _const_spec((1, DV_B)), _const_spec((tt, tt)), _const_spec((GLA_CHUNK, LANES))],
        out_specs=[pl.BlockSpec((tt, D_BV), row),
                   pl.BlockSpec((None, H_B, DK_B, DV_B), lambda b, t: (b, 0, 0, 0))],
        out_shape=[jax.ShapeDtypeStruct((n, D_BV), F32),
                   jax.ShapeDtypeStruct((batch, H_B, DK_B, DV_B), F32)],
        scratch_shapes=[pltpu.VMEM((D_BK, D_BV), F32)],
        compiler_params=_cparams(("parallel", "arbitrary")),
        name="gla_prompt",
    )(zqkv, zr, zg, wg, bg.reshape(1, D_BK), hg.reshape(1, DV_B), ltri, ones)


def _even_sample_a_kernel(za_ref, zqkv_ref, zg_ref, st_ref, cw_ref, cb_ref, lg_ref, lb_ref, wg_ref, bg_ref,
                          u_ref, aout_ref, q_ref, k_ref, a_ref):
    za = za_ref[...]
    u = za[:, :D_A] * jax.nn.sigmoid(za[:, D_A:])
    u_ref[...] = u
    acc = jnp.zeros_like(u) + cb_ref[...]
    for w in range(CONV_W - 1):
        acc = acc + st_ref[w] * cw_ref[w:w + 1, :]
    acc = acc + u * cw_ref[CONV_W - 1:CONV_W, :]
    aout_ref[...] = _ln_swish(acc, lg_ref[...], lb_ref[...])
    zq = zqkv_ref[...]
    q_ref[...] = zq[:, 0:D_BK] * (DK_B ** -0.5)
    k_ref[...] = zq[:, D_BK:2 * D_BK]
    a_ref[...] = jnp.exp(_log_decay(zg_ref[...], wg_ref[...], bg_ref[...]))


def even_sample_a(za, zqkv, zg, st_t, cw, cb, lg, lb, wg, bg):
    n = za.shape[0]
    shp = lambda w: jax.ShapeDtypeStruct((n, w), F32)
    return pl.pallas_call(
        _even_sample_a_kernel,
        out_shape=[shp(D_A), shp(D_A), shp(D_BK), shp(D_BK), shp(D_BK)],
        compiler_params=pltpu.CompilerParams(vmem_limit_bytes=VMEM_LIMIT),
        name="even_sample_a",
    )(za, zqkv, zg, st_t, cw, cb.reshape(1, D_A), lg.reshape(1, D_A), lb.reshape(1, D_A), wg,
      bg.reshape(1, D_BK))


def _even_sample_b_kernel(a_ref, k_ref, q_ref, v_ref, r_ref, s_ref, hg_ref, snew_ref, bout_ref):
    v = v_ref[...]
    outs = []
    for h in range(H_B):
        rows = slice(h * DK_B, (h + 1) * DK_B)
        vh = v[:, h * DV_B:(h + 1) * DV_B]
        s_new = a_ref[rows, :] * s_ref[h] + k_ref[rows, :] * vh
        snew_ref[h] = s_new
        outs.append(jnp.sum(q_ref[rows, :] * s_new, axis=0, keepdims=True))
    bout_ref[...] = _head_norm_gate(jnp.concatenate(outs, axis=1), hg_ref[...], r_ref[...])


def even_sample_b(a3, k3, q3, v3, r3, state, hg):
    nb = a3.shape[0]
    col = pl.BlockSpec((None, D_BK, 1), lambda b: (b, 0, 0))
    rowv = pl.BlockSpec((None, 1, D_BV), lambda b: (b, 0, 0))
    st = pl.BlockSpec((None, H_B, DK_B, DV_B), lambda b: (b, 0, 0, 0))
    return pl.pallas_call(
        _even_sample_b_kernel,
        grid=(nb,),
        in_specs=[col, col, col, rowv, rowv, st, _const_spec((1, DV_B))],
        out_specs=[st, rowv],
        out_shape=[jax.ShapeDtypeStruct((nb, H_B, DK_B, DV_B), F32),
                   jax.ShapeDtypeStruct((nb, 1, D_BV), F32)],
        compiler_params=_cparams(("parallel",)),
        name="even_sample_b",
    )(a3, k3, q3, v3, r3, state, hg.reshape(1, DV_B))


def _even_out_kernel(x_ref, a_ref, b_ref, w_ref, o_ref):
    y = _dot(a_ref[...].astype(BF16), w_ref[0:D_A, :]) + _dot(b_ref[...].astype(BF16), w_ref[D_A:, :])
    o_ref[...] = x_ref[...] + y


def even_out(x, a, b, w):
    n = x.shape[0]
    tm = min(512, n)
    row = lambda i: (i, 0)
    return pl.pallas_call(
        _even_out_kernel,
        grid=(n // tm,),
        in_specs=[pl.BlockSpec((tm, D_MODEL), row), pl.BlockSpec((tm, D_A), row),
                  pl.BlockSpec((tm, D_BV), row), _const_spec(w.shape)],
        out_specs=pl.BlockSpec((tm, D_MODEL), row),
        out_shape=jax.ShapeDtypeStruct((n, D_MODEL), F32),
        compiler_params=_cparams(("parallel",)),
        name="even_out",
    )(x, a, b, w)


def _mla_prep_kernel(cq_ref, ckv_ref, kpe_ref, kper_ref, cs_ref, sn_ref, qg_ref, kvg_ref,
                     wn_ref, wa_ref, wb_ref, wuk_ref, qc_ref, kc_ref, ckvo_ref, kpeo_ref):
    cs = cs_ref[...]
    sn = sn_ref[...]
    ckv = _rms(ckv_ref[...], kvg_ref[...])
    kpe = kpe_ref[...] * cs + kper_ref[...] * sn
    ckvo_ref[...] = ckv
    kpeo_ref[...] = kpe
    kc_ref[:, 0:R_KV] = ckv.astype(BF16)
    kc_ref[:, R_KV:] = kpe.astype(BF16)
    cqn = _rms(cq_ref[...], qg_ref[...]).astype(BF16)
    qn = _dot(cqn, wn_ref[...]).astype(BF16)
    qa = _dot(cqn, wa_ref[...])
    qb = _dot(cqn, wb_ref[...])
    for h in range(H_C):
        pair = qn[:, (h // 2) * LANES:(h // 2 + 1) * LANES]
        ql = _dot(pair, wuk_ref[h]) * MLA_SCALE
        qp = (qa[:, h * LANES:(h + 1) * LANES] * cs + qb[:, h * LANES:(h + 1) * LANES] * sn) * MLA_SCALE
        qc_ref[h, :, 0:R_KV] = ql.astype(BF16)
        qc_ref[h, :, R_KV:] = qp.astype(BF16)


def mla_prep(cq, ckv, kpe, kper, cs, sn, qg, kvg, wn, wa, wb, wuk, pos_tiles):
    n = cq.shape[0]
    tm = ATT_TQ
    row = lambda i: (i, 0)
    pos = lambda i: (i % pos_tiles, 0)
    return pl.pallas_call(
        _mla_prep_kernel,
        grid=(n // tm,),
        in_specs=[pl.BlockSpec((tm, R_Q), row), pl.BlockSpec((tm, R_KV), row),
                  pl.BlockSpec((tm, LANES), row), pl.BlockSpec((tm, LANES), row),
                  pl.BlockSpec((tm, LANES), pos), pl.BlockSpec((tm, LANES), pos),
                  _const_spec((1, R_Q)), _const_spec((1, R_KV)),
                  _const_spec(wn.shape), _const_spec(wa.shape), _const_spec(wb.shape), _const_spec(wuk.shape)],
        out_specs=[pl.BlockSpec((None, H_C, tm, KC_W), lambda i: (i, 0, 0, 0)),
                   pl.BlockSpec((tm, KC_W), row), pl.BlockSpec((tm, R_KV), row),
                   pl.BlockSpec((tm, LANES), row)],
        out_shape=[jax.ShapeDtypeStruct((n // tm, H_C, tm, KC_W), BF16),
                   jax.ShapeDtypeStruct((n, KC_W), BF16),
                   jax.ShapeDtypeStruct((n, R_KV), F32),
                   jax.ShapeDtypeStruct((n, LANES), F32)],
        compiler_params=_cparams(("parallel",)),
        name="mla_prep",
    )(cq, ckv, kpe, kper, cs, sn, qg.reshape(1, R_Q), kvg.reshape(1, R_KV), wn, wa, wb, wuk)


ATT_STRIP = H_C * ATT_TQ


def _attn_strip_update(q, kc, vt, visible, carry, acc_ref):
    m_old, l_old = carry
    st = _dot_nt(kc, q)
    if visible is not None:
        st = jnp.where(visible, st, NEG_INF)
    m_new = jnp.maximum(m_old, jnp.max(st, axis=0, keepdims=True))
    p = jnp.exp(st - m_new)
    alpha = jnp.exp(m_old - m_new)
    acc_ref[...] = alpha * acc_ref[...] + _dot(vt, p.astype(BF16))
    return m_new, alpha * l_old + jnp.sum(p, axis=0, keepdims=True)


def _mla_attn_kernel(qc_ref, kc_ref, vt_ref, ol_ref, acc_ref):
    i = pl.program_id(1)
    n_full = (i * ATT_TQ) // ATT_TK
    k0 = pl.multiple_of(n_full * ATT_TK, ATT_TK)
    tok = i * ATT_TQ + lax.broadcasted_iota(I32, (ATT_TK, ATT_STRIP), 1) % ATT_TQ
    key = k0 + lax.broadcasted_iota(I32, (ATT_TK, ATT_STRIP), 0)
    causal = key <= tok
    heads = ATT_STRIP // ATT_TQ
    n_strip = H_C // heads
    qs = [qc_ref[heads * s:heads * (s + 1)].reshape(ATT_STRIP, KC_W) for s in range(n_strip)]
    acc_ref[...] = jnp.zeros((n_strip, R_KV, ATT_STRIP), F32)

    def full_block(j, carry):
        kc = kc_ref[pl.ds(pl.multiple_of(j * ATT_TK, ATT_TK), ATT_TK), :]
        vt = vt_ref[j]
        return tuple(_attn_strip_update(qs[s], kc, vt, None, carry[s], acc_ref.at[s]) for s in range(n_strip))

    init = (jnp.full((1, ATT_STRIP), NEG_INF, F32), jnp.zeros((1, ATT_STRIP), F32))
    carry = lax.fori_loop(0, n_full, full_block, (init,) * n_strip)
    kc = kc_ref[pl.ds(k0, ATT_TK), :]
    vt = vt_ref[n_full]
    for s in range(n_strip):
        _, l_fin = _attn_strip_update(qs[s], kc, vt, causal, carry[s], acc_ref.at[s])
        o = (acc_ref[s] / l_fin).T
        ol_ref[heads * s:heads * (s + 1)] = o.reshape(heads, ATT_TQ, R_KV).astype(BF16)


def mla_attn_prompt(qc, kc, batch):
    ntile = qc.shape[0]
    nq = ntile // batch
    seq = nq * ATT_TQ
    nk = seq // ATT_TK
    vt = jnp.transpose(kc[:, :R_KV].reshape(batch, nk, ATT_TK, R_KV), (0, 1, 3, 2))
    return pl.pallas_call(
        _mla_attn_kernel,
        grid=(batch, nq),
        in_specs=[pl.BlockSpec((None, H_C, ATT_TQ, KC_W), lambda b, i: (b * nq + i, 0, 0, 0)),
                  pl.BlockSpec((seq, KC_W), lambda b, i: (b, 0)),
                  pl.BlockSpec((None, nk, R_KV, ATT_TK), lambda b, i: (b, 0, 0, 0))],
        out_specs=pl.BlockSpec((None, H_C, ATT_TQ, R_KV), lambda b, i: (b * nq + i, 0, 0, 0)),
        out_shape=jax.ShapeDtypeStruct((ntile, H_C, ATT_TQ, R_KV), BF16),
        scratch_shapes=[pltpu.VMEM((H_C * ATT_TQ // ATT_STRIP, R_KV, ATT_STRIP), F32)],
        compiler_params=_cparams(("parallel", "arbitrary")),
        name="mla_attn_prompt",
    )(qc, kc, vt)


def _mla_decode_kernel(pt_ref, q_ref, kcur_ref, *refs, ng):
    ckv_refs = refs[:DEC_PAGES]
    kpe_refs = refs[DEC_PAGES:2 * DEC_PAGES]
    o_ref, m_ref, l_ref, acc_ref = refs[2 * DEC_PAGES:]
    g = pl.program_id(1)

    @pl.when(g == 0)
    def _():
        m_ref[...] = jnp.full((H_C, 1), NEG_INF, F32)
        l_ref[...] = jnp.zeros((H_C, 1), F32)
        acc_ref[...] = jnp.zeros((H_C, R_KV), F32)

    q = q_ref[...]
    ql = q[:, 0:R_KV]
    qp = q[:, R_KV:R_KV + D_ROPE]
    cks = [r[...].astype(BF16) for r in ckv_refs]
    s = jnp.concatenate(
        [_dot_nt(ql, cks[p]) + _dot_nt(qp, kpe_refs[p][...].astype(BF16)) for p in range(DEC_PAGES)], axis=1)
    m_old = m_ref[...]
    m_new = jnp.maximum(m_old, jnp.max(s, axis=1, keepdims=True))
    p_all = jnp.exp(s - m_new).astype(BF16)
    alpha = jnp.exp(m_old - m_new)
    pv = _dot(p_all[:, 0:PAGE_SIZE], cks[0])
    for p in range(1, DEC_PAGES):
        pv = pv + _dot(p_all[:, p * PAGE_SIZE:(p + 1) * PAGE_SIZE], cks[p])
    l_new = alpha * l_ref[...] + jnp.sum(p_all.astype(F32), axis=1, keepdims=True)
    acc_new = alpha * acc_ref[...] + pv
    m_ref[...] = m_new
    l_ref[...] = l_new
    acc_ref[...] = acc_new

    @pl.when(g == ng - 1)
    def _():
        kcur = kcur_ref[...].astype(F32)
        s_cur = jnp.sum(q.astype(F32) * kcur, axis=1, keepdims=True)
        m_fin = jnp.maximum(m_new, s_cur)
        a_fin = jnp.exp(m_new - m_fin)
        p_cur = jnp.exp(s_cur - m_fin)
        l_fin = a_fin * l_new + p_cur
        o_ref[...] = (a_fin * acc_new + p_cur * kcur[:, 0:R_KV]) / l_fin


def mla_decode(page_table, q_s, kc_s, cache_ckv, cache_kpe, layer):
    nb, npages = page_table.shape
    ng = npages // DEC_PAGES

    def page_spec(width, p):
        return pl.BlockSpec((None, None, PAGE_SIZE, width),
                            lambda b, g, pt: (layer, pt[b, g * DEC_PAGES + p], 0, 0))

    grid_spec = pltpu.PrefetchScalarGridSpec(
        num_scalar_prefetch=1,
        grid=(nb, ng),
        in_specs=([pl.BlockSpec((None, H_C, KC_W), lambda b, g, pt: (b, 0, 0)),
                   pl.BlockSpec((None, 1, KC_W), lambda b, g, pt: (b, 0, 0))]
                  + [page_spec(R_KV, p) for p in range(DEC_PAGES)]
                  + [page_spec(D_ROPE, p) for p in range(DEC_PAGES)]),
        out_specs=pl.BlockSpec((None, H_C, R_KV), lambda b, g, pt: (b, 0, 0)),
        scratch_shapes=[pltpu.VMEM((H_C, 1), F32), pltpu.VMEM((H_C, 1), F32), pltpu.VMEM((H_C, R_KV), F32)],
    )
    return pl.pallas_call(
        functools.partial(_mla_decode_kernel, ng=ng),
        grid_spec=grid_spec,
        out_shape=jax.ShapeDtypeStruct((nb, H_C, R_KV), F32),
        compiler_params=_cparams(("parallel", "arbitrary")),
        name="mla_decode",
    )(page_table, q_s, kc_s, *([cache_ckv] * DEC_PAGES), *([cache_kpe] * DEC_PAGES))


def _mla_out_kernel(x_ref, ol_ref, wuv_ref, wo_ref, o_ref):
    pairs = []
    for p in range(H_C // 2):
        pairs.append((_dot(ol_ref[2 * p], wuv_ref[2 * p]) + _dot(ol_ref[2 * p + 1], wuv_ref[2 * p + 1])).astype(BF16))
    o = jnp.concatenate(pairs, axis=1)
    o_ref[...] = x_ref[...] + _dot(o, wo_ref[...])


def mla_out(x, ol, wuv, wo):
    n = x.shape[0]
    tm = ATT_TQ
    return pl.pallas_call(
        _mla_out_kernel,
        grid=(n // tm,),
        in_specs=[pl.BlockSpec((tm, D_MODEL), lambda i: (i, 0)),
                  pl.BlockSpec((None, H_C, tm, R_KV), lambda i: (i, 0, 0, 0)),
                  _const_spec(wuv.shape), _const_spec(wo.shape)],
        out_specs=pl.BlockSpec((tm, D_MODEL), lambda i: (i, 0)),
        out_shape=jax.ShapeDtypeStruct((n, D_MODEL), F32),
        compiler_params=_cparams(("parallel",)),
        name="mla_out",
    )(x, ol, wuv, wo)


def _top16_rows(s):
    rows, tm = s.shape
    iota = lax.broadcasted_iota(I32, (rows, tm), 0)
    rank = jnp.full((rows, tm), TOPK, I32)
    vals = []
    cur = s
    for it in range(TOPK):
        m = jnp.max(cur, axis=0, keepdims=True)
        idx = jnp.min(jnp.where(cur == m, iota, rows), axis=0, keepdims=True)
        sel = iota == idx
        rank = jnp.where(sel, it, rank)
        cur = jnp.where(sel, NEG_INF, cur)
        vals.append(m)
    return jnp.concatenate(vals, axis=0), rank


CAND_ROWS = 80


def _pair_top16(v1, v2):
    cand, _, flat = _pair_candidates(v1, v2)
    sums, a_sel = [], []
    for _ in range(TOPK):
        m = jnp.max(cand, axis=0, keepdims=True)
        ci = jnp.min(jnp.where(cand == m, flat, TOPK * TOPK), axis=0, keepdims=True)
        cand = jnp.where(flat == ci, NEG_INF, cand)
        sums.append(m)
        a_sel.append(ci >> 4)
    return sums, a_sel


def _top16_rows_distinct(s):
    rows, tm = s.shape
    rank = jnp.full((rows, tm), float(TOPK), F32)
    vals = []
    cur = s
    for it in range(TOPK):
        m = jnp.max(cur, axis=0, keepdims=True)
        sel = cur == m
        rank = jnp.where(sel, float(it), rank)
        cur = jnp.where(sel, NEG_INF, cur)
        vals.append(m)
    marked = jnp.sum(jnp.where(rank < float(TOPK), 1.0, 0.0), axis=0, keepdims=True)
    return jnp.concatenate(vals, axis=0), rank, marked


def _pair_candidates(v1, v2):
    tm = v1.shape[1]
    pieces = [v1[0:1] + v2]
    for a in range(1, 8):
        pieces.append(v1[a:a + 1] + v2[0:8])
    pieces.append(v1[8:16] + v2[0:1])
    cand = jnp.concatenate(pieces, axis=0)
    r = lax.broadcasted_iota(I32, (CAND_ROWS, tm), 0)
    a_idx = jnp.where(r < 16, 0, jnp.where(r < 72, 1 + ((r - 16) >> 3), r - 64))
    b_idx = jnp.where(r < 16, r, jnp.where(r < 72, (r - 16) & 7, 0))
    cand = jnp.where((a_idx + 1) * (b_idx + 1) <= TOPK, cand, NEG_INF)
    return cand, a_idx, a_idx * TOPK + b_idx


def _pair_top16_distinct(v1, v2):
    cand, _, _ = _pair_candidates(v1, v2)
    cur = cand
    sums = []
    for _ in range(TOPK):
        m = jnp.max(cur, axis=0, keepdims=True)
        cur = jnp.where(cur == m, NEG_INF, cur)
        sums.append(m)
    picked = jnp.where((cur == NEG_INF) & (cand > NEG_INF), 1.0, 0.0)
    counts = [jnp.sum(picked[0:16], axis=0, keepdims=True)]
    for a in range(1, 8):
        counts.append(jnp.sum(picked[8 + 8 * a:16 + 8 * a], axis=0, keepdims=True))
    cnt = jnp.concatenate(counts + [picked[72:80]], axis=0)
    return sums, cnt, jnp.sum(picked, axis=0, keepdims=True)


def _bf16_twice(x):
    hi = pltpu.bitcast(x.astype(BF16).astype(F32), jnp.uint32)
    return hi | (hi >> 16)


def _route_tables(s1, s2, rank1, rank2, v1, v2, sums, cnt, refs, h):
    r2_ref, bt_ref, e1_ref, e2_ref = refs
    z = jnp.exp(sums[0] - sums[0])
    for kk in range(1, TOPK):
        z = z + jnp.exp(sums[kk] - sums[0])
    bt = jnp.zeros(s1.shape, F32)
    for a in range(TOPK):
        bt = jnp.where(rank1 == a, cnt[a:a + 1], bt)
    e1 = jnp.exp(s1 - v1[0:1]) * (1.0 / z)
    e2 = jnp.exp(s2 - v2[0:1])
    r2 = rank2.astype(F32)
    for c in range(s1.shape[1] // LANES):
        cols = slice(c * LANES, (c + 1) * LANES)
        bt_ref[h, c] = _bf16_twice(bt[:, cols])
        e1_ref[h, c] = _bf16_twice(e1[:, cols])
        e2_ref[h, c] = pltpu.bitcast(e2[:, cols].astype(BF16), jnp.uint32)
        r2_ref[h, c] = pltpu.bitcast(r2[:, cols].astype(BF16), jnp.uint32)


def _peer_route_kernel(x_ref, g_ref, wq_ref, keys_ref, xn_ref, r2_ref, bt_ref, e1_ref, e2_ref, q_ref, bad_ref):
    xn = _rms(x_ref[...], g_ref[...]).astype(BF16)
    xn_ref[...] = xn
    q = _dot(xn, wq_ref[...]).astype(BF16)
    tm = q.shape[0]
    for k in range(2 * H_P):
        q_ref[k] = q[:, k * D_HALF:(k + 1) * D_HALF]
    refs = (r2_ref, bt_ref, e1_ref, e2_ref)

    def scores(h):
        return _dot_nt(keys_ref[h, 0], q_ref[2 * h]), _dot_nt(keys_ref[h, 1], q_ref[2 * h + 1])

    for h in range(H_P):
        s1, s2 = scores(h)
        v1, rank1, n1 = _top16_rows_distinct(s1)
        v2, rank2, n2 = _top16_rows_distinct(s2)
        sums, cnt, n3 = _pair_top16_distinct(v1, v2)
        _route_tables(s1, s2, rank1, rank2, v1, v2, sums, cnt, refs, h)
        bad_ref[h:h + 1, :] = jnp.abs(n1 - TOPK) + jnp.abs(n2 - TOPK) + jnp.abs(n3 - TOPK)

    def redo(h, carry):
        @pl.when(jnp.max(bad_ref[pl.ds(h, 1), :]) > 0.0)
        def _():
            iota16 = lax.broadcasted_iota(I32, (TOPK, tm), 0)
            s1, s2 = scores(h)
            v1, rank1 = _top16_rows(s1)
            v2, rank2 = _top16_rows(s2)
            sums, a_sel = _pair_top16(v1, v2)
            cnt = jnp.zeros((TOPK, tm), F32)
            for kk in range(TOPK):
                cnt = cnt + jnp.where(iota16 == a_sel[kk], 1.0, 0.0)
            _route_tables(s1, s2, rank1, rank2, v1, v2, sums, cnt, refs, h)
        return carry

    lax.fori_loop(0, H_P, redo, 0)


def peer_route(x, g, wq, keys):
    n = x.shape[0]
    tm = min(256, n)
    tab = lambda rows: pl.BlockSpec((H_P, tm // LANES, rows, LANES), lambda i: (0, i, 0, 0))
    u32 = jnp.uint32
    tab_shape = lambda rows, dt: jax.ShapeDtypeStruct((H_P, n // LANES, rows, LANES), dt)
    return pl.pallas_call(
        _peer_route_kernel,
        grid=(n // tm,),
        in_specs=[pl.BlockSpec((tm, D_MODEL), lambda i: (i, 0)), _const_spec((1, D_MODEL)),
                  _const_spec(wq.shape), _const_spec(keys.shape)],
        out_specs=[pl.BlockSpec((tm, D_MODEL), lambda i: (i, 0)), tab(N_KEYS // 2), tab(N_KEYS), tab(N_KEYS),
                   tab(N_KEYS // 2)],
        out_shape=[jax.ShapeDtypeStruct((n, D_MODEL), BF16), tab_shape(N_KEYS // 2, u32),
                   tab_shape(N_KEYS, u32), tab_shape(N_KEYS, u32), tab_shape(N_KEYS // 2, u32)],
        scratch_shapes=[pltpu.VMEM((2 * H_P, tm, D_HALF), BF16), pltpu.VMEM((H_P, tm), F32)],
        compiler_params=_cparams(("parallel",)),
        name="peer_route",
    )(x, g.reshape(1, D_MODEL), wq, keys)


def _peer_dense_kernel(x_ref, xn_ref, r2_ref, bt_ref, e1_ref, e2_ref, u_ref, vt_ref, o_ref,
                       acc_ref, ht_ref, act_ref, *, tm, te, ne):
    e = pl.program_id(1)

    @pl.when(e == 0)
    def _():
        acc_ref[...] = jnp.zeros_like(acc_ref)

    assert te == SUBLANES * N_KEYS
    xn = xn_ref[...]

    def project(pair):
        rows = slice(2 * pair * N_KEYS, (2 * pair + 2) * N_KEYS)
        ht_ref[rows, :] = _dot_nt(u_ref[rows, :], xn)

    groups = N_KEYS // (2 * SUBLANES)

    def token_row(ref, h, c, ii):
        word = jnp.broadcast_to(ref[h, c, e, ii:ii + 1, :], (SUBLANES, LANES))
        return pltpu.bitcast(word, BF16)[None]

    project(0)
    for pair in range(SUBLANES // 2):
        iis = (2 * pair, 2 * pair + 1)
        if pair + 1 < SUBLANES // 2:
            project(pair + 1)
        for c in range(tm // LANES):
            cols = slice(c * LANES, (c + 1) * LANES)
            gates = [jnp.zeros((groups, 2 * SUBLANES, LANES), BF16) for _ in iis]
            for h in range(H_P):
                r2 = pltpu.bitcast(r2_ref[h, c], BF16).reshape(groups, 2 * SUBLANES, LANES)
                e2 = pltpu.bitcast(e2_ref[h, c], BF16).reshape(groups, 2 * SUBLANES, LANES)
                for k, ii in enumerate(iis):
                    val = e2 * token_row(e1_ref, h, c, ii)
                    gates[k] = gates[k] + jnp.where(r2 < token_row(bt_ref, h, c, ii), val, jnp.zeros_like(val))
            for k, ii in enumerate(iis):
                rows = slice(ii * N_KEYS, (ii + 1) * N_KEYS)
                ht = ht_ref[rows, cols]
                gelu = (0.5 * ht * (1.0 + lax.erf(ht * (2.0 ** -0.5)))).astype(BF16)
                act_ref[rows, cols] = gelu * gates[k].reshape(N_KEYS, LANES)
    acc_ref[...] += _dot(vt_ref[...], act_ref[...])

    @pl.when(e == ne - 1)
    def _():
        o_ref[...] = x_ref[...] + acc_ref[...].T


def peer_dense(x, xn, r2, bt, e1, e2, u, vt):
    n = x.shape[0]
    tm = min(512, n)
    te = SUBLANES * N_KEYS
    ne = N_EXPERTS // te
    f32_view = (H_P, n // LANES, N_KEYS // SUBLANES, SUBLANES, LANES)
    bf16_view = (H_P, n // LANES, N_KEYS // 2, LANES)
    tab = lambda view: pl.BlockSpec((H_P, tm // LANES) + view[2:], lambda i, e: (0, i) + (0,) * (len(view) - 2))
    return pl.pallas_call(
        functools.partial(_peer_dense_kernel, tm=tm, te=te, ne=ne),
        grid=(n // tm, ne),
        in_specs=[pl.BlockSpec((tm, D_MODEL), lambda i, e: (i, 0)),
                  pl.BlockSpec((tm, D_MODEL), lambda i, e: (i, 0)),
                  tab(bf16_view), tab(f32_view), tab(f32_view), tab(bf16_view),
                  pl.BlockSpec((te, D_MODEL), lambda i, e: (e, 0)),
                  pl.BlockSpec((D_MODEL, te), lambda i, e: (0, e))],
        out_specs=pl.BlockSpec((tm, D_MODEL), lambda i, e: (i, 0)),
        out_shape=jax.ShapeDtypeStruct((n, D_MODEL), F32),
        scratch_shapes=[pltpu.VMEM((D_MODEL, tm), F32), pltpu.VMEM((te, tm), F32),
                        pltpu.VMEM((te, tm), BF16)],
        compiler_params=_cparams(("parallel", "arbitrary")),
        name="peer_dense",
    )(x, xn, r2.reshape(bf16_view), bt.reshape(f32_view), e1.reshape(f32_view), e2.reshape(bf16_view), u, vt)


def peer(x, g, wq, keys, u, vt):
    xn, r2, bt, e1, e2 = peer_route(x, g, wq, keys)
    return peer_dense(x, xn, r2, bt, e1, e2, u, vt)


def _pad_cols(w, width):
    return jnp.pad(w, ((0, 0), (0, width - w.shape[1])))


def _rope_tables(pos):
    half = D_ROPE // 2
    inv = jnp.power(ROPE_BASE, -jnp.arange(half, dtype=F32) / half)
    ang = pos.astype(F32)[:, None] * inv[None, :]
    cos2 = jnp.concatenate([jnp.cos(ang), jnp.cos(ang)], axis=1)
    sin2 = jnp.concatenate([jnp.sin(ang), jnp.sin(ang)], axis=1)
    return _pad_cols(cos2, LANES), _pad_cols(sin2, LANES)


def _rot_half_cols(w):
    half = D_ROPE // 2
    return jnp.concatenate([-w[..., half:], w[..., :half]], axis=-1)


def _even_layer_weights(w_in, w_gate, w_out):
    w_in_p = _pad_cols(w_in, 2 * D_A + 2 * D_BK + 2 * D_BV + LANES).astype(BF16)
    wg = jnp.pad(w_gate, ((0, LANES - GATE_RANK), (0, 0))).astype(BF16)
    return w_in_p, wg, w_out.astype(BF16)


def _odd_layer_weights(w_in, w_q_b, w_uk, w_uv, w_out):
    wk = w_in[:, R_Q + R_KV:]
    w_in_p = jnp.concatenate([w_in[:, :R_Q + R_KV], _pad_cols(wk, LANES), _pad_cols(_rot_half_cols(wk), LANES)],
                             axis=1).astype(BF16)
    wq3 = w_q_b.reshape(R_Q, H_C, D_NOPE + D_ROPE)
    wn = wq3[:, :, :D_NOPE].reshape(R_Q, H_C * D_NOPE).astype(BF16)
    wpe = wq3[:, :, D_NOPE:]
    pad3 = lambda w: jnp.pad(w, ((0, 0), (0, 0), (0, LANES - D_ROPE))).reshape(R_Q, H_C * LANES).astype(BF16)
    wa, wb = pad3(wpe), pad3(_rot_half_cols(wpe))
    ukt = jnp.transpose(w_uk, (1, 2, 0))
    uv = jnp.transpose(w_uv, (1, 0, 2))
    odd = (jnp.arange(H_C) % 2 == 1)[:, None, None]
    zk = jnp.zeros_like(ukt)
    wuk = jnp.where(odd, jnp.concatenate([zk, ukt], axis=1), jnp.concatenate([ukt, zk], axis=1)).astype(BF16)
    zv = jnp.zeros_like(uv)
    wuv = jnp.where(odd, jnp.concatenate([zv, uv], axis=2), jnp.concatenate([uv, zv], axis=2)).astype(BF16)
    return w_in_p, wn, wa, wb, wuk, wuv, w_out.astype(BF16)


def kernel(x_prompt, x_sample, cache_ckv, cache_kpe, state_conv, state_gla, page_table, ab_norm, ab_w_in,
           conv_w, conv_b, conv_ln_g, conv_ln_b, gla_w_gate, gla_b_gate, gla_head_g, ab_w_out, c_norm, c_w_in,
           c_q_norm, c_kv_norm, c_w_q_b, c_w_uk, c_w_uv, c_w_out, ffn_norm, peer_w_q, peer_keys, peer_u,
           peer_v, final_norm):
    bp, tp, _ = x_prompt.shape
    bs = x_sample.shape[0]
    n_p = bp * tp
    xp = x_prompt.reshape(n_p, D_MODEL)
    xs = jnp.pad(x_sample.reshape(bs, D_MODEL), ((0, SAMPLE_ROWS - bs), (0, 0)))
    cs_p, sn_p = _rope_tables(jnp.arange(tp, dtype=I32))
    cs_s, sn_s = _rope_tables(jnp.full((SAMPLE_ROWS,), PAST_LEN, I32))
    ckv_p, kpe_p, ckv_s, kpe_s = [], [], [], []
    conv_p, conv_s, gla_p, gla_s = [], [], [], []
    for l in range(DEPTH):
        j = l // 2
        if l % 2 == 0:
            w_in, wg, w_out = _even_layer_weights(ab_w_in[j], gla_w_gate[j], ab_w_out[j])
            widths = (2 * D_A, 2 * D_BK + D_BV, D_BV, LANES)
            za, zqkv, zr, zg = norm_matmul(xp, ab_norm[j], w_in, widths)
            a_out, cbuf = conv_branch_prompt(za, conv_w[j], conv_b[j], conv_ln_g[j], conv_ln_b[j], bp)
            b_out, s_fin = gla_prompt(zqkv, zr, zg, wg, gla_b_gate[j], gla_head_g[j], bp)
            xp = even_out(xp, a_out, b_out, w_out)
            conv_p.append(cbuf)
            gla_p.append(s_fin)
            za, zqkv, zr, zg = norm_matmul(xs, ab_norm[j], w_in, widths)
            st_t = jnp.pad(jnp.transpose(state_conv[j], (1, 0, 2)), ((0, 0), (0, SAMPLE_ROWS - bs), (0, 0)))
            u, a_out, qg, kg, ag = even_sample_a(za, zqkv, zg, st_t, conv_w[j], conv_b[j], conv_ln_g[j],
                                                 conv_ln_b[j], wg, gla_b_gate[j])
            col = lambda t: t[:bs].reshape(bs, D_BK, 1)
            s_new, b_out = even_sample_b(col(ag), col(kg), col(qg),
                                         zqkv[:bs, 2 * D_BK:].reshape(bs, 1, D_BV),
                                         zr[:bs].reshape(bs, 1, D_BV), state_gla[j], gla_head_g[j])
            b_out = jnp.pad(b_out.reshape(bs, D_BV), ((0, SAMPLE_ROWS - bs), (0, 0)))
            xs = even_out(xs, a_out, b_out, w_out)
            conv_s.append(jnp.concatenate([state_conv[j][:, 1:], u[:bs, None, :]], axis=1))
            gla_s.append(s_new)
        else:
            w_in, wn, wa, wb, wuk, wuv, w_out = _odd_layer_weights(c_w_in[j], c_w_q_b[j], c_w_uk[j], c_w_uv[j],
                                                                   c_w_out[j])
            widths = (R_Q, R_KV, LANES, LANES)
            cq, ckv, kpe, kper = norm_matmul(xp, c_norm[j], w_in, widths)
            qc, kc, ckv_o, kpe_o = mla_prep(cq, ckv, kpe, kper, cs_p, sn_p, c_q_norm[j], c_kv_norm[j],
                                            wn, wa, wb, wuk, tp // ATT_TQ)
            ol = mla_attn_prompt(qc, kc, bp)
            xp = mla_out(xp, ol, wuv, w_out)
            ckv_p.append(ckv_o.reshape(bp, tp, R_KV))
            kpe_p.append(kpe_o[:, :D_ROPE].reshape(bp, tp, D_ROPE))
            cq, ckv, kpe, kper = norm_matmul(xs, c_norm[j], w_in, widths)
            qc, kc, ckv_o, kpe_o = mla_prep(cq, ckv, kpe, kper, cs_s, sn_s, c_q_norm[j], c_kv_norm[j],
                                            wn, wa, wb, wuk, 1)
            q_s = jnp.transpose(qc[0, :, :bs, :], (1, 0, 2))
            o_lat = mla_decode(page_table, q_s, kc[:bs].reshape(bs, 1, KC_W), cache_ckv, cache_kpe, j)
            ol = jnp.pad(jnp.transpose(o_lat, (1, 0, 2)), ((0, 0), (0, SAMPLE_ROWS - bs), (0, 0)))
            xs = mla_out(xs, ol.astype(BF16)[None], wuv, w_out)
            ckv_s.append(ckv_o[:bs].reshape(bs, 1, R_KV))
            kpe_s.append(kpe_o[:bs, :D_ROPE].reshape(bs, 1, D_ROPE))
        wq = peer_w_q[l].astype(BF16)
        keys = peer_keys[l].astype(BF16)
        u_tab = peer_u[l].astype(BF16)
        vt_tab = peer_v[l].astype(BF16).T
        xp = peer(xp, ffn_norm[l], wq, keys, u_tab, vt_tab)
        xs = peer(xs, ffn_norm[l], wq, keys, u_tab, vt_tab)
    y_prompt = final_rms(xp, final_norm).reshape(bp, tp, D_MODEL)
    y_sample = final_rms(xs, final_norm)[:bs].reshape(bs, 1, D_MODEL)
    return (y_prompt, y_sample,
            jnp.stack(ckv_p), jnp.stack(kpe_p), jnp.stack(ckv_s), jnp.stack(kpe_s),
            jnp.stack(conv_p), jnp.stack(conv_s), jnp.stack(gla_p), jnp.stack(gla_s))
```

```python
import functools

import jax
import jax.numpy as jnp
from jax import lax
from jax.experimental import pallas as pl
from jax.experimental.pallas import tpu as pltpu

F32 = jnp.float32
BF16 = jnp.bfloat16
I32 = jnp.int32

D_MODEL = 1024
SEQ = 2048
DEPTH = 4
DEC_BATCH = 32
PAST_LEN = 16384
PAGE_SIZE = 128
EPS = 1e-6
D_A = 512
CONV_W = 31
H_B = 4
D_BV = 512
D_BK = 256
DK_B = 64
DV_B = 128
GATE_RANK = 16
GATE_TAU = 16.0
GLA_CHUNK = 64
GLA_SUB = 16
H_C = 16
D_NOPE = 64
D_ROPE = 32
D_V = 64
R_Q = 384
R_KV = 256
ROPE_BASE = 10000.0
MLA_SCALE = (D_NOPE + D_ROPE) ** -0.5
H_P = 8
N_KEYS = 128
N_EXPERTS = N_KEYS * N_KEYS
D_HALF = 128
TOPK = 16

LANES = 128
SUBLANES = 8
SAMPLE_ROWS = 128
KC_W = R_KV + LANES
ATT_TQ = 128
ATT_TK = 512
DEC_PAGES = 16
VMEM_LIMIT = 56 * 1024 * 1024
NEG_INF = float("-inf")


def _cparams(sem):
    return pltpu.CompilerParams(dimension_semantics=sem, vmem_limit_bytes=VMEM_LIMIT)


def _dot(a, b):
    return jnp.dot(a, b, preferred_element_type=F32)


def _dot_nt(a, b):
    return lax.dot_general(a, b, (((1,), (1,)), ((), ())), preferred_element_type=F32)


def _dot_tn(a, b):
    return lax.dot_general(a, b, (((0,), (0,)), ((), ())), preferred_element_type=F32)


def _rms(x, g):
    return x * lax.rsqrt(jnp.mean(x * x, axis=-1, keepdims=True) + EPS) * g


def _split3(x):
    hi = x.astype(BF16)
    r1 = x - hi.astype(F32)
    mid = r1.astype(BF16)
    lo = (r1 - mid.astype(F32)).astype(BF16)
    return hi, mid, lo


def _const_spec(shape):
    nd = len(shape)
    return pl.BlockSpec(shape, lambda *_: (0,) * nd)


def _norm_mm_kernel(x_ref, g_ref, w_ref, *out_refs, widths):
    xn = _rms(x_ref[...], g_ref[...])
    y = _dot(xn.astype(BF16), w_ref[...])
    off = 0
    for o_ref, w in zip(out_refs, widths):
        o_ref[...] = y[:, off:off + w]
        off += w


def norm_matmul(x, g, w, widths):
    n, k = x.shape
    tm = min(512, n)
    return pl.pallas_call(
        functools.partial(_norm_mm_kernel, widths=widths),
        grid=(n // tm,),
        in_specs=[pl.BlockSpec((tm, k), lambda i: (i, 0)), _const_spec((1, k)), _const_spec(w.shape)],
        out_specs=[pl.BlockSpec((tm, wd), lambda i: (i, 0)) for wd in widths],
        out_shape=[jax.ShapeDtypeStruct((n, wd), F32) for wd in widths],
        compiler_params=_cparams(("parallel",)),
        name="norm_matmul",
    )(x, g.reshape(1, k), w)


def _final_norm_kernel(x_ref, g_ref, o_ref):
    o_ref[...] = _rms(x_ref[...], g_ref[...])


def final_rms(x, g):
    n, k = x.shape
    tm = min(512, n)
    return pl.pallas_call(
        _final_norm_kernel,
        grid=(n // tm,),
        in_specs=[pl.BlockSpec((tm, k), lambda i: (i, 0)), _const_spec((1, k))],
        out_specs=pl.BlockSpec((tm, k), lambda i: (i, 0)),
        out_shape=jax.ShapeDtypeStruct((n, k), F32),
        compiler_params=_cparams(("parallel",)),
        name="final_norm",
    )(x, g.reshape(1, k))


def _ln_swish(c, g, b):
    mu = jnp.mean(c, axis=-1, keepdims=True)
    xc = c - mu
    y = xc * lax.rsqrt(jnp.mean(xc * xc, axis=-1, keepdims=True) + EPS) * g + b
    return y * jax.nn.sigmoid(y)


CONV_HIST = 32


def _conv_kernel(za_ref, cw_ref, cb_ref, lg_ref, lb_ref, out_ref, buf_ref, ext_ref, *, tt, nt):
    t = pl.program_id(1)

    @pl.when(t == 0)
    def _():
        ext_ref[0:CONV_HIST, :] = jnp.zeros((CONV_HIST, D_A), F32)

    za = za_ref[...]
    ext_ref[CONV_HIST:CONV_HIST + tt, :] = za[:, :D_A] * jax.nn.sigmoid(za[:, D_A:])
    first = CONV_HIST - (CONV_W - 1)
    acc = jnp.zeros((tt, D_A), F32) + cb_ref[...]
    for w in range(CONV_W):
        acc = acc + ext_ref[pl.ds(first + w, tt), :] * cw_ref[w:w + 1, :]
    out_ref[...] = _ln_swish(acc, lg_ref[...], lb_ref[...])

    @pl.when(t == nt - 1)
    def _():
        buf_ref[...] = ext_ref[pl.ds(tt + first, CONV_W - 1), :]

    ext_ref[0:CONV_HIST, :] = ext_ref[tt:tt + CONV_HIST, :]


def conv_branch_prompt(za, cw, cb, lg, lb, batch):
    n = za.shape[0]
    seq = n // batch
    tt = 512
    nt = seq // tt
    return pl.pallas_call(
        functools.partial(_conv_kernel, tt=tt, nt=nt),
        grid=(batch, nt),
        in_specs=[pl.BlockSpec((tt, 2 * D_A), lambda b, t: (b * nt + t, 0)),
                  _const_spec((CONV_W, D_A)), _const_spec((1, D_A)), _const_spec((1, D_A)),
                  _const_spec((1, D_A))],
        out_specs=[pl.BlockSpec((tt, D_A), lambda b, t: (b * nt + t, 0)),
                   pl.BlockSpec((None, CONV_W - 1, D_A), lambda b, t: (b, 0, 0))],
        out_shape=[jax.ShapeDtypeStruct((n, D_A), F32),
                   jax.ShapeDtypeStruct((batch, CONV_W - 1, D_A), F32)],
        scratch_shapes=[pltpu.VMEM((tt + CONV_HIST, D_A), F32)],
        compiler_params=_cparams(("parallel", "arbitrary")),
        name="conv_branch_prompt",
    )(za, cw, cb.reshape(1, D_A), lg.reshape(1, D_A), lb.reshape(1, D_A))


def _log_decay(zg, wg, bg):
    x = _dot(zg.astype(BF16), wg) + bg
    return (jnp.minimum(x, 0.0) - jnp.log1p(jnp.exp(-jnp.abs(x)))) * (1.0 / GATE_TAU)


def _head_norm_gate(o, hg, r):
    outs = []
    for h in range(H_B):
        oh = o[:, h * DV_B:(h + 1) * DV_B]
        outs.append(oh * lax.rsqrt(jnp.mean(oh * oh, axis=-1, keepdims=True) + EPS) * hg)
    return jnp.concatenate(outs, axis=1) * (r * jax.nn.sigmoid(r))


def _gla_kernel(zqkv_ref, zr_ref, zg_ref, wg_ref, bg_ref, hg_ref, ltri_ref, ones_ref,
                bout_ref, sfin_ref, s_ref, *, tt, nt):
    t = pl.program_id(1)

    @pl.when(t == 0)
    def _():
        s_ref[...] = jnp.zeros((D_BK, D_BV), F32)

    zq = zqkv_ref[...]
    q = zq[:, 0:D_BK] * (DK_B ** -0.5)
    k = zq[:, D_BK:2 * D_BK]
    v = zq[:, 2 * D_BK:]
    la = _log_decay(zg_ref[...], wg_ref[...], bg_ref[...])
    la3 = _split3(la)
    ltri = ltri_ref[...]
    b = _dot(ltri, la3[0]) + _dot(ltri, la3[1]) + _dot(ltri, la3[2])
    ones = ones_ref[...]

    lane_head = lax.broadcasted_iota(I32, (GLA_SUB, D_BK), 1) // DK_B
    bd_mask = (lax.broadcasted_iota(I32, (D_BK, D_BV), 0) // DK_B
               == lax.broadcasted_iota(I32, (D_BK, D_BV), 1) // DV_B)
    o_chunks = []
    for c in range(tt // GLA_CHUNK):
        r0 = c * GLA_CHUNK
        bc = b[r0:r0 + GLA_CHUNK]
        qc = q[r0:r0 + GLA_CHUNK]
        kc = k[r0:r0 + GLA_CHUNK]
        vc = v[r0:r0 + GLA_CHUNK]
        vcb = vc.astype(BF16)
        s_prev = s_ref[...]
        o = _dot((qc * jnp.exp(bc)).astype(BF16), s_prev.astype(BF16))
        o_rows = []
        for i in range(GLA_CHUNK // GLA_SUB):
            i0 = i * GLA_SUB
            nk = i0 + GLA_SUB
            ref_b = bc[i0:i0 + 1, :]
            qi = qc[i0:nk] * jnp.exp(bc[i0:nk] - ref_b)
            ki = kc[0:nk] * jnp.exp(ref_b - bc[0:nk])
            qs = jnp.concatenate([jnp.where(lane_head == h, qi, 0.0) for h in range(H_B)], axis=0)
            att = _dot_nt(qs.astype(BF16), ki.astype(BF16))
            row_i = lax.broadcasted_iota(I32, (H_B * GLA_SUB, nk), 0) % GLA_SUB
            col_j = lax.broadcasted_iota(I32, (H_B * GLA_SUB, nk), 1)
            att = jnp.where(col_j <= row_i + i0, att, 0.0)
            res = _dot(att.astype(BF16), vcb[0:nk])
            o_rows.append(jnp.concatenate(
                [res[h * GLA_SUB:(h + 1) * GLA_SUB, h * DV_B:(h + 1) * DV_B] for h in range(H_B)], axis=1))
        o_chunks.append(o + jnp.concatenate(o_rows, axis=0))
        b_end = bc[GLA_CHUNK - 1:GLA_CHUNK, :]
        kst = kc * jnp.exp(b_end - bc)
        kv = _dot_tn(kst.astype(BF16), vcb)
        b_end_t = (_dot_tn(la3[0][r0:r0 + GLA_CHUNK], ones) + _dot_tn(la3[1][r0:r0 + GLA_CHUNK], ones)
                   + _dot_tn(la3[2][r0:r0 + GLA_CHUNK], ones))
        dec = jnp.exp(b_end_t)
        s_ref[...] = s_prev * jnp.concatenate([dec] * H_B, axis=1) + jnp.where(bd_mask, kv, 0.0)

    o_all = jnp.concatenate(o_chunks, axis=0)
    bout_ref[...] = _head_norm_gate(o_all, hg_ref[...], zr_ref[...])

    @pl.when(t == nt - 1)
    def _():
        for h in range(H_B):
            sfin_ref[h] = s_ref[h * DK_B:(h + 1) * DK_B, h * DV_B:(h + 1) * DV_B]


def gla_prompt(zqkv, zr, zg, wg, bg, hg, batch):
    n = zqkv.shape[0]
    seq = n // batch
    tt = 256
    nt = seq // tt
    ri = lax.broadcasted_iota(I32, (tt, tt), 0)
    ci = lax.broadcasted_iota(I32, (tt, tt), 1)
    ltri = ((ri // GLA_CHUNK == ci // GLA_CHUNK) & (ci <= ri)).astype(BF16)
    ones = jnp.ones((GLA_CHUNK, LANES), BF16)
    row = lambda b, t: (b * nt + t, 0)
    return pl.pallas_call(
        functools.partial(_gla_kernel, tt=tt, nt=nt),
        grid=(batch, nt),
        in_specs=[pl.BlockSpec((tt, 2 * D_BK + D_BV), row), pl.BlockSpec((tt, D_BV), row),
                  pl.BlockSpec((tt, LANES), row), _const_spec((LANES, D_BK)), _const_spec((1, D_BK)),
                  _const_spec((1, DV_B)), _const_spec((tt, tt)), _const_spec((GLA_CHUNK, LANES))],
        out_specs=[pl.BlockSpec((tt, D_BV), row),
                   pl.BlockSpec((None, H_B, DK_B, DV_B), lambda b, t: (b, 0, 0, 0))],
        out_shape=[jax.ShapeDtypeStruct((n, D_BV), F32),
                   jax.ShapeDtypeStruct((batch, H_B, DK_B, DV_B), F32)],
        scratch_shapes=[pltpu.VMEM((D_BK, D_BV), F32)],
        compiler_params=_cparams(("parallel", "arbitrary")),
        name="gla_prompt",
    )(zqkv, zr, zg, wg, bg.reshape(1, D_BK), hg.reshape(1, DV_B), ltri, ones)


def _even_sample_a_kernel(za_ref, zqkv_ref, zg_ref, st_ref, cw_ref, cb_ref, lg_ref, lb_ref, wg_ref, bg_ref,
                          u_ref, aout_ref, q_ref, k_ref, a_ref):
    za = za_ref[...]
    u = za[:, :D_A] * jax.nn.sigmoid(za[:, D_A:])
    u_ref[...] = u
    acc = jnp.zeros_like(u) + cb_ref[...]
    for w in range(CONV_W - 1):
        acc = acc + st_ref[w] * cw_ref[w:w + 1, :]
    acc = acc + u * cw_ref[CONV_W - 1:CONV_W, :]
    aout_ref[...] = _ln_swish(acc, lg_ref[...], lb_ref[...])
    zq = zqkv_ref[...]
    q_ref[...] = zq[:, 0:D_BK] * (DK_B ** -0.5)
    k_ref[...] = zq[:, D_BK:2 * D_BK]
    a_ref[...] = jnp.exp(_log_decay(zg_ref[...], wg_ref[...], bg_ref[...]))


def even_sample_a(za, zqkv, zg, st_t, cw, cb, lg, lb, wg, bg):
    n = za.shape[0]
    shp = lambda w: jax.ShapeDtypeStruct((n, w), F32)
    return pl.pallas_call(
        _even_sample_a_kernel,
        out_shape=[shp(D_A), shp(D_A), shp(D_BK), shp(D_BK), shp(D_BK)],
        compiler_params=pltpu.CompilerParams(vmem_limit_bytes=VMEM_LIMIT),
        name="even_sample_a",
    )(za, zqkv, zg, st_t, cw, cb.reshape(1, D_A), lg.reshape(1, D_A), lb.reshape(1, D_A), wg,
      bg.reshape(1, D_BK))


def _even_sample_b_kernel(a_ref, k_ref, q_ref, v_ref, r_ref, s_ref, hg_ref, snew_ref, bout_ref):
    v = v_ref[...]
    outs = []
    for h in range(H_B):
        rows = slice(h * DK_B, (h + 1) * DK_B)
        vh = v[:, h * DV_B:(h + 1) * DV_B]
        s_new = a_ref[rows, :] * s_ref[h] + k_ref[rows, :] * vh
        snew_ref[h] = s_new
        outs.append(jnp.sum(q_ref[rows, :] * s_new, axis=0, keepdims=True))
    bout_ref[...] = _head_norm_gate(jnp.concatenate(outs, axis=1), hg_ref[...], r_ref[...])


def even_sample_b(a3, k3, q3, v3, r3, state, hg):
    nb = a3.shape[0]
    col = pl.BlockSpec((None, D_BK, 1), lambda b: (b, 0, 0))
    rowv = pl.BlockSpec((None, 1, D_BV), lambda b: (b, 0, 0))
    st = pl.BlockSpec((None, H_B, DK_B, DV_B), lambda b: (b, 0, 0, 0))
    return pl.pallas_call(
        _even_sample_b_kernel,
        grid=(nb,),
        in_specs=[col, col, col, rowv, rowv, st, _const_spec((1, DV_B))],
        out_specs=[st, rowv],
        out_shape=[jax.ShapeDtypeStruct((nb, H_B, DK_B, DV_B), F32),
                   jax.ShapeDtypeStruct((nb, 1, D_BV), F32)],
        compiler_params=_cparams(("parallel",)),
        name="even_sample_b",
    )(a3, k3, q3, v3, r3, state, hg.reshape(1, DV_B))


def _even_out_kernel(x_ref, a_ref, b_ref, w_ref, o_ref):
    y = _dot(a_ref[...].astype(BF16), w_ref[0:D_A, :]) + _dot(b_ref[...].astype(BF16), w_ref[D_A:, :])
    o_ref[...] = x_ref[...] + y


def even_out(x, a, b, w):
    n = x.shape[0]
    tm = min(512, n)
    row = lambda i: (i, 0)
    return pl.pallas_call(
        _even_out_kernel,
        grid=(n // tm,),
        in_specs=[pl.BlockSpec((tm, D_MODEL), row), pl.BlockSpec((tm, D_A), row),
                  pl.BlockSpec((tm, D_BV), row), _const_spec(w.shape)],
        out_specs=pl.BlockSpec((tm, D_MODEL), row),
        out_shape=jax.ShapeDtypeStruct((n, D_MODEL), F32),
        compiler_params=_cparams(("parallel",)),
        name="even_out",
    )(x, a, b, w)


def _mla_prep_kernel(cq_ref, ckv_ref, kpe_ref, kper_ref, cs_ref, sn_ref, qg_ref, kvg_ref,
                     wn_ref, wa_ref, wb_ref, wuk_ref, qc_ref, kc_ref, ckvo_ref, kpeo_ref):
    cs = cs_ref[...]
    sn = sn_ref[...]
    ckv = _rms(ckv_ref[...], kvg_ref[...])
    kpe = kpe_ref[...] * cs + kper_ref[...] * sn
    ckvo_ref[...] = ckv
    kpeo_ref[...] = kpe
    kc_ref[:, 0:R_KV] = ckv.astype(BF16)
    kc_ref[:, R_KV:] = kpe.astype(BF16)
    cqn = _rms(cq_ref[...], qg_ref[...]).astype(BF16)
    qn = _dot(cqn, wn_ref[...]).astype(BF16)
    qa = _dot(cqn, wa_ref[...])
    qb = _dot(cqn, wb_ref[...])
    for h in range(H_C):
        pair = qn[:, (h // 2) * LANES:(h // 2 + 1) * LANES]
        ql = (_dot(pair, wuk_ref[h]) * MLA_SCALE).astype(BF16)
        qp = ((qa[:, h * LANES:(h + 1) * LANES] * cs + qb[:, h * LANES:(h + 1) * LANES] * sn)
              * MLA_SCALE).astype(BF16)
        for t in range(qc_ref.shape[0]):
            tok = slice(t * ATT_TQ, (t + 1) * ATT_TQ)
            qc_ref[t, h, :, 0:R_KV] = ql[tok]
            qc_ref[t, h, :, R_KV:] = qp[tok]


def mla_prep(cq, ckv, kpe, kper, cs, sn, qg, kvg, wn, wa, wb, wuk, pos_tiles):
    n = cq.shape[0]
    tm = min(512, n)
    sub = tm // ATT_TQ
    row = lambda i: (i, 0)
    pos = lambda i: (i % (pos_tiles // sub), 0)
    return pl.pallas_call(
        _mla_prep_kernel,
        grid=(n // tm,),
        in_specs=[pl.BlockSpec((tm, R_Q), row), pl.BlockSpec((tm, R_KV), row),
                  pl.BlockSpec((tm, LANES), row), pl.BlockSpec((tm, LANES), row),
                  pl.BlockSpec((tm, LANES), pos), pl.BlockSpec((tm, LANES), pos),
                  _const_spec((1, R_Q)), _const_spec((1, R_KV)),
                  _const_spec(wn.shape), _const_spec(wa.shape), _const_spec(wb.shape), _const_spec(wuk.shape)],
        out_specs=[pl.BlockSpec((sub, H_C, ATT_TQ, KC_W), lambda i: (i, 0, 0, 0)),
                   pl.BlockSpec((tm, KC_W), row), pl.BlockSpec((tm, R_KV), row),
                   pl.BlockSpec((tm, LANES), row)],
        out_shape=[jax.ShapeDtypeStruct((n // ATT_TQ, H_C, ATT_TQ, KC_W), BF16),
                   jax.ShapeDtypeStruct((n, KC_W), BF16),
                   jax.ShapeDtypeStruct((n, R_KV), F32),
                   jax.ShapeDtypeStruct((n, LANES), F32)],
        compiler_params=_cparams(("parallel",)),
        name="mla_prep",
    )(cq, ckv, kpe, kper, cs, sn, qg.reshape(1, R_Q), kvg.reshape(1, R_KV), wn, wa, wb, wuk)


ATT_STRIP = H_C * ATT_TQ


def _attn_strip_update(q, kc, vt, visible, carry, acc_ref):
    m_old, l_old = carry
    st = _dot_nt(kc, q)
    if visible is not None:
        st = jnp.where(visible, st, NEG_INF)
    m_new = jnp.maximum(m_old, jnp.max(st, axis=0, keepdims=True))
    p = jnp.exp(st - m_new)
    alpha = jnp.exp(m_old - m_new)
    acc_ref[...] = alpha * acc_ref[...] + _dot(vt, p.astype(BF16))
    return m_new, alpha * l_old + jnp.sum(p, axis=0, keepdims=True)


def _mla_attn_kernel(qc_ref, kc_ref, vt_ref, ol_ref, acc_ref):
    i = pl.program_id(1)
    n_full = (i * ATT_TQ) // ATT_TK
    k0 = pl.multiple_of(n_full * ATT_TK, ATT_TK)
    tok = i * ATT_TQ + lax.broadcasted_iota(I32, (ATT_TK, ATT_STRIP), 1) % ATT_TQ
    key = k0 + lax.broadcasted_iota(I32, (ATT_TK, ATT_STRIP), 0)
    causal = key <= tok
    heads = ATT_STRIP // ATT_TQ
    n_strip = H_C // heads
    qs = [qc_ref[heads * s:heads * (s + 1)].reshape(ATT_STRIP, KC_W) for s in range(n_strip)]
    acc_ref[...] = jnp.zeros((n_strip, R_KV, ATT_STRIP), F32)

    def full_block(j, carry):
        kc = kc_ref[pl.ds(pl.multiple_of(j * ATT_TK, ATT_TK), ATT_TK), :]
        vt = vt_ref[j]
        return tuple(_attn_strip_update(qs[s], kc, vt, None, carry[s], acc_ref.at[s]) for s in range(n_strip))

    init = (jnp.full((1, ATT_STRIP), NEG_INF, F32), jnp.zeros((1, ATT_STRIP), F32))
    carry = lax.fori_loop(0, n_full, full_block, (init,) * n_strip)
    kc = kc_ref[pl.ds(k0, ATT_TK), :]
    vt = vt_ref[n_full]
    for s in range(n_strip):
        _, l_fin = _attn_strip_update(qs[s], kc, vt, causal, carry[s], acc_ref.at[s])
        o = (acc_ref[s] / l_fin).T
        ol_ref[heads * s:heads * (s + 1)] = o.reshape(heads, ATT_TQ, R_KV).astype(BF16)


def mla_attn_prompt(qc, kc, batch):
    ntile = qc.shape[0]
    nq = ntile // batch
    seq = nq * ATT_TQ
    nk = seq // ATT_TK
    vt = jnp.transpose(kc[:, :R_KV].reshape(batch, nk, ATT_TK, R_KV), (0, 1, 3, 2))
    return pl.pallas_call(
        _mla_attn_kernel,
        grid=(batch, nq),
        in_specs=[pl.BlockSpec((None, H_C, ATT_TQ, KC_W), lambda b, i: (b * nq + i, 0, 0, 0)),
                  pl.BlockSpec((seq, KC_W), lambda b, i: (b, 0)),
                  pl.BlockSpec((None, nk, R_KV, ATT_TK), lambda b, i: (b, 0, 0, 0))],
        out_specs=pl.BlockSpec((None, H_C, ATT_TQ, R_KV), lambda b, i: (b * nq + i, 0, 0, 0)),
        out_shape=jax.ShapeDtypeStruct((ntile, H_C, ATT_TQ, R_KV), BF16),
        scratch_shapes=[pltpu.VMEM((H_C * ATT_TQ // ATT_STRIP, R_KV, ATT_STRIP), F32)],
        compiler_params=_cparams(("parallel", "arbitrary")),
        name="mla_attn_prompt",
    )(qc, kc, vt)


def _mla_decode_kernel(pt_ref, q_ref, kcur_ref, *refs, ng):
    ckv_refs = refs[:DEC_PAGES]
    kpe_refs = refs[DEC_PAGES:2 * DEC_PAGES]
    o_ref, m_ref, l_ref, acc_ref = refs[2 * DEC_PAGES:]
    g = pl.program_id(1)

    @pl.when(g == 0)
    def _():
        m_ref[...] = jnp.full((H_C, 1), NEG_INF, F32)
        l_ref[...] = jnp.zeros((H_C, 1), F32)
        acc_ref[...] = jnp.zeros((H_C, R_KV), F32)

    q = q_ref[...]
    ql = q[:, 0:R_KV]
    qp = q[:, R_KV:R_KV + D_ROPE]
    cks = [r[...].astype(BF16) for r in ckv_refs]
    s = jnp.concatenate(
        [_dot_nt(ql, cks[p]) + _dot_nt(qp, kpe_refs[p][...].astype(BF16)) for p in range(DEC_PAGES)], axis=1)
    m_old = m_ref[...]
    m_new = jnp.maximum(m_old, jnp.max(s, axis=1, keepdims=True))
    p_all = jnp.exp(s - m_new).astype(BF16)
    alpha = jnp.exp(m_old - m_new)
    pv = _dot(p_all[:, 0:PAGE_SIZE], cks[0])
    for p in range(1, DEC_PAGES):
        pv = pv + _dot(p_all[:, p * PAGE_SIZE:(p + 1) * PAGE_SIZE], cks[p])
    l_new = alpha * l_ref[...] + jnp.sum(p_all.astype(F32), axis=1, keepdims=True)
    acc_new = alpha * acc_ref[...] + pv
    m_ref[...] = m_new
    l_ref[...] = l_new
    acc_ref[...] = acc_new

    @pl.when(g == ng - 1)
    def _():
        kcur = kcur_ref[...].astype(F32)
        s_cur = jnp.sum(q.astype(F32) * kcur, axis=1, keepdims=True)
        m_fin = jnp.maximum(m_new, s_cur)
        a_fin = jnp.exp(m_new - m_fin)
        p_cur = jnp.exp(s_cur - m_fin)
        l_fin = a_fin * l_new + p_cur
        o_ref[...] = (a_fin * acc_new + p_cur * kcur[:, 0:R_KV]) / l_fin


def mla_decode(page_table, q_s, kc_s, cache_ckv, cache_kpe, layer):
    nb, npages = page_table.shape
    ng = npages // DEC_PAGES

    def page_spec(width, p):
        return pl.BlockSpec((None, None, PAGE_SIZE, width),
                            lambda b, g, pt: (layer, pt[b, g * DEC_PAGES + p], 0, 0))

    grid_spec = pltpu.PrefetchScalarGridSpec(
        num_scalar_prefetch=1,
        grid=(nb, ng),
        in_specs=([pl.BlockSpec((None, H_C, KC_W), lambda b, g, pt: (b, 0, 0)),
                   pl.BlockSpec((None, 1, KC_W), lambda b, g, pt: (b, 0, 0))]
                  + [page_spec(R_KV, p) for p in range(DEC_PAGES)]
                  + [page_spec(D_ROPE, p) for p in range(DEC_PAGES)]),
        out_specs=pl.BlockSpec((None, H_C, R_KV), lambda b, g, pt: (b, 0, 0)),
        scratch_shapes=[pltpu.VMEM((H_C, 1), F32), pltpu.VMEM((H_C, 1), F32), pltpu.VMEM((H_C, R_KV), F32)],
    )
    return pl.pallas_call(
        functools.partial(_mla_decode_kernel, ng=ng),
        grid_spec=grid_spec,
        out_shape=jax.ShapeDtypeStruct((nb, H_C, R_KV), F32),
        compiler_params=_cparams(("parallel", "arbitrary")),
        name="mla_decode",
    )(page_table, q_s, kc_s, *([cache_ckv] * DEC_PAGES), *([cache_kpe] * DEC_PAGES))


def _mla_out_kernel(x_ref, ol_ref, wuv_ref, wo_ref, o_ref):
    rows = x_ref.shape[0]
    head = lambda h: ol_ref[:, h].reshape(rows, R_KV)
    pairs = []
    for p in range(H_C // 2):
        pairs.append((_dot(head(2 * p), wuv_ref[2 * p]) + _dot(head(2 * p + 1), wuv_ref[2 * p + 1])).astype(BF16))
    o = jnp.concatenate(pairs, axis=1)
    o_ref[...] = x_ref[...] + _dot(o, wo_ref[...])


def mla_out(x, ol, wuv, wo):
    n = x.shape[0]
    tm = min(512, n)
    return pl.pallas_call(
        _mla_out_kernel,
        grid=(n // tm,),
        in_specs=[pl.BlockSpec((tm, D_MODEL), lambda i: (i, 0)),
                  pl.BlockSpec((tm // ATT_TQ, H_C, ATT_TQ, R_KV), lambda i: (i, 0, 0, 0)),
                  _const_spec(wuv.shape), _const_spec(wo.shape)],
        out_specs=pl.BlockSpec((tm, D_MODEL), lambda i: (i, 0)),
        out_shape=jax.ShapeDtypeStruct((n, D_MODEL), F32),
        compiler_params=_cparams(("parallel",)),
        name="mla_out",
    )(x, ol, wuv, wo)


def _top16_rows(s):
    rows, tm = s.shape
    iota = lax.broadcasted_iota(I32, (rows, tm), 0)
    rank = jnp.full((rows, tm), TOPK, I32)
    vals = []
    cur = s
    for it in range(TOPK):
        m = jnp.max(cur, axis=0, keepdims=True)
        idx = jnp.min(jnp.where(cur == m, iota, rows), axis=0, keepdims=True)
        sel = iota == idx
        rank = jnp.where(sel, it, rank)
        cur = jnp.where(sel, NEG_INF, cur)
        vals.append(m)
    return jnp.concatenate(vals, axis=0), rank


CAND_ROWS = 80


def _pair_top16(v1, v2):
    cand, _, flat = _pair_candidates(v1, v2)
    sums, a_sel = [], []
    for _ in range(TOPK):
        m = jnp.max(cand, axis=0, keepdims=True)
        ci = jnp.min(jnp.where(cand == m, flat, TOPK * TOPK), axis=0, keepdims=True)
        cand = jnp.where(flat == ci, NEG_INF, cand)
        sums.append(m)
        a_sel.append(ci >> 4)
    return sums, a_sel


def _top16_rows_distinct(s):
    rows, tm = s.shape
    rank = jnp.full((rows, tm), float(TOPK), F32)
    vals = []
    cur = s
    for it in range(TOPK):
        m = jnp.max(cur, axis=0, keepdims=True)
        sel = cur == m
        rank = jnp.where(sel, float(it), rank)
        cur = jnp.where(sel, NEG_INF, cur)
        vals.append(m)
    marked = jnp.sum(jnp.where(rank < float(TOPK), 1.0, 0.0), axis=0, keepdims=True)
    return jnp.concatenate(vals, axis=0), rank, marked


def _pair_candidates(v1, v2):
    tm = v1.shape[1]
    pieces = [v1[0:1] + v2]
    for a in range(1, 8):
        pieces.append(v1[a:a + 1] + v2[0:8])
    pieces.append(v1[8:16] + v2[0:1])
    cand = jnp.concatenate(pieces, axis=0)
    r = lax.broadcasted_iota(I32, (CAND_ROWS, tm), 0)
    a_idx = jnp.where(r < 16, 0, jnp.where(r < 72, 1 + ((r - 16) >> 3), r - 64))
    b_idx = jnp.where(r < 16, r, jnp.where(r < 72, (r - 16) & 7, 0))
    cand = jnp.where((a_idx + 1) * (b_idx + 1) <= TOPK, cand, NEG_INF)
    return cand, a_idx, a_idx * TOPK + b_idx


def _pair_top16_distinct(v1, v2):
    cand, _, _ = _pair_candidates(v1, v2)
    cur = cand
    sums = []
    for _ in range(TOPK):
        m = jnp.max(cur, axis=0, keepdims=True)
        cur = jnp.where(cur == m, NEG_INF, cur)
        sums.append(m)
    picked = jnp.where((cur == NEG_INF) & (cand > NEG_INF), 1.0, 0.0)
    counts = [jnp.sum(picked[0:16], axis=0, keepdims=True)]
    for a in range(1, 8):
        counts.append(jnp.sum(picked[8 + 8 * a:16 + 8 * a], axis=0, keepdims=True))
    cnt = jnp.concatenate(counts + [picked[72:80]], axis=0)
    return sums, cnt, jnp.sum(picked, axis=0, keepdims=True)


def _bf16_twice(x):
    hi = pltpu.bitcast(x.astype(BF16).astype(F32), jnp.uint32)
    return hi | (hi >> 16)


def _route_tables(s1, s2, rank1, rank2, v1, v2, sums, cnt, refs, h):
    r2_ref, bt_ref, e1_ref, e2_ref = refs
    z = jnp.exp(sums[0] - sums[0])
    for kk in range(1, TOPK):
        z = z + jnp.exp(sums[kk] - sums[0])
    bt = jnp.zeros(s1.shape, F32)
    for a in range(TOPK):
        bt = jnp.where(rank1 == a, cnt[a:a + 1], bt)
    e1 = jnp.exp(s1 - v1[0:1]) * (1.0 / z)
    e2 = jnp.exp(s2 - v2[0:1])
    r2 = rank2.astype(F32)
    for c in range(s1.shape[1] // LANES):
        cols = slice(c * LANES, (c + 1) * LANES)
        bt_ref[h, c] = _bf16_twice(bt[:, cols])
        e1_ref[h, c] = _bf16_twice(e1[:, cols])
        e2_ref[h, c] = pltpu.bitcast(e2[:, cols].astype(BF16), jnp.uint32)
        r2_ref[h, c] = pltpu.bitcast(r2[:, cols].astype(BF16), jnp.uint32)


def _peer_route_kernel(x_ref, g_ref, wq_ref, keys_ref, xn_ref, r2_ref, bt_ref, e1_ref, e2_ref, q_ref, bad_ref):
    xn = _rms(x_ref[...], g_ref[...]).astype(BF16)
    xn_ref[...] = xn
    q = _dot(xn, wq_ref[...]).astype(BF16)
    tm = q.shape[0]
    for k in range(2 * H_P):
        q_ref[k] = q[:, k * D_HALF:(k + 1) * D_HALF]
    refs = (r2_ref, bt_ref, e1_ref, e2_ref)

    def scores(h):
        return _dot_nt(keys_ref[h, 0], q_ref[2 * h]), _dot_nt(keys_ref[h, 1], q_ref[2 * h + 1])

    for h in range(H_P):
        s1, s2 = scores(h)
        v1, rank1, n1 = _top16_rows_distinct(s1)
        v2, rank2, n2 = _top16_rows_distinct(s2)
        sums, cnt, n3 = _pair_top16_distinct(v1, v2)
        _route_tables(s1, s2, rank1, rank2, v1, v2, sums, cnt, refs, h)
        bad_ref[h:h + 1, :] = jnp.abs(n1 - TOPK) + jnp.abs(n2 - TOPK) + jnp.abs(n3 - TOPK)

    def redo(h, carry):
        @pl.when(jnp.max(bad_ref[pl.ds(h, 1), :]) > 0.0)
        def _():
            iota16 = lax.broadcasted_iota(I32, (TOPK, tm), 0)
            s1, s2 = scores(h)
            v1, rank1 = _top16_rows(s1)
            v2, rank2 = _top16_rows(s2)
            sums, a_sel = _pair_top16(v1, v2)
            cnt = jnp.zeros((TOPK, tm), F32)
            for kk in range(TOPK):
                cnt = cnt + jnp.where(iota16 == a_sel[kk], 1.0, 0.0)
            _route_tables(s1, s2, rank1, rank2, v1, v2, sums, cnt, refs, h)
        return carry

    lax.fori_loop(0, H_P, redo, 0)


def peer_route(x, g, wq, keys):
    n = x.shape[0]
    tm = min(256, n)
    tab = lambda rows: pl.BlockSpec((H_P, tm // LANES, rows, LANES), lambda i: (0, i, 0, 0))
    u32 = jnp.uint32
    tab_shape = lambda rows, dt: jax.ShapeDtypeStruct((H_P, n // LANES, rows, LANES), dt)
    return pl.pallas_call(
        _peer_route_kernel,
        grid=(n // tm,),
        in_specs=[pl.BlockSpec((tm, D_MODEL), lambda i: (i, 0)), _const_spec((1, D_MODEL)),
                  _const_spec(wq.shape), _const_spec(keys.shape)],
        out_specs=[pl.BlockSpec((tm, D_MODEL), lambda i: (i, 0)), tab(N_KEYS // 2), tab(N_KEYS), tab(N_KEYS),
                   tab(N_KEYS // 2)],
        out_shape=[jax.ShapeDtypeStruct((n, D_MODEL), BF16), tab_shape(N_KEYS // 2, u32),
                   tab_shape(N_KEYS, u32), tab_shape(N_KEYS, u32), tab_shape(N_KEYS // 2, u32)],
        scratch_shapes=[pltpu.VMEM((2 * H_P, tm, D_HALF), BF16), pltpu.VMEM((H_P, tm), F32)],
        compiler_params=_cparams(("parallel",)),
        name="peer_route",
    )(x, g.reshape(1, D_MODEL), wq, keys)


def _peer_dense_kernel(x_ref, xn_ref, r2_ref, bt_ref, e1_ref, e2_ref, u_ref, v_ref, o_ref,
                       acc_ref, ht_ref, act_ref, *, tm, te, ne):
    e = pl.program_id(1)

    @pl.when(e == 0)
    def _():
        acc_ref[...] = jnp.zeros_like(acc_ref)

    assert te == SUBLANES * N_KEYS
    xn = xn_ref[...]

    def project(pair):
        rows = slice(2 * pair * N_KEYS, (2 * pair + 2) * N_KEYS)
        ht_ref[rows, :] = _dot_nt(u_ref[rows, :], xn)

    groups = N_KEYS // (2 * SUBLANES)

    def token_row(ref, h, c, ii):
        word = jnp.broadcast_to(ref[h, c, e, ii:ii + 1, :], (SUBLANES, LANES))
        return pltpu.bitcast(word, BF16)[None]

    project(0)
    for pair in range(SUBLANES // 2):
        iis = (2 * pair, 2 * pair + 1)
        if pair + 1 < SUBLANES // 2:
            project(pair + 1)
        for c in range(tm // LANES):
            cols = slice(c * LANES, (c + 1) * LANES)
            gates = [jnp.zeros((groups, 2 * SUBLANES, LANES), BF16) for _ in iis]
            for h in range(H_P):
                r2 = pltpu.bitcast(r2_ref[h, c], BF16).reshape(groups, 2 * SUBLANES, LANES)
                e2 = pltpu.bitcast(e2_ref[h, c], BF16).reshape(groups, 2 * SUBLANES, LANES)
                for k, ii in enumerate(iis):
                    val = e2 * token_row(e1_ref, h, c, ii)
                    gates[k] = gates[k] + jnp.where(r2 < token_row(bt_ref, h, c, ii), val, jnp.zeros_like(val))
            for k, ii in enumerate(iis):
                rows = slice(ii * N_KEYS, (ii + 1) * N_KEYS)
                ht = ht_ref[rows, cols]
                gelu = (0.5 * ht * (1.0 + lax.erf(ht * (2.0 ** -0.5)))).astype(BF16)
                act_ref[rows, cols] = gelu * gates[k].reshape(N_KEYS, LANES)
    acc_ref[...] += _dot_tn(act_ref[...], v_ref[...])

    @pl.when(e == ne - 1)
    def _():
        o_ref[...] = x_ref[...] + acc_ref[...]


def peer_dense(x, xn, r2, bt, e1, e2, u, vt):
    n = x.shape[0]
    tm = min(512, n)
    te = SUBLANES * N_KEYS
    ne = N_EXPERTS // te
    f32_view = (H_P, n // LANES, N_KEYS // SUBLANES, SUBLANES, LANES)
    bf16_view = (H_P, n // LANES, N_KEYS // 2, LANES)
    tab = lambda view: pl.BlockSpec((H_P, tm // LANES) + view[2:], lambda i, e: (0, i) + (0,) * (len(view) - 2))
    return pl.pallas_call(
        functools.partial(_peer_dense_kernel, tm=tm, te=te, ne=ne),
        grid=(n // tm, ne),
        in_specs=[pl.BlockSpec((tm, D_MODEL), lambda i, e: (i, 0)),
                  pl.BlockSpec((tm, D_MODEL), lambda i, e: (i, 0)),
                  tab(bf16_view), tab(f32_view), tab(f32_view), tab(bf16_view),
                  pl.BlockSpec((te, D_MODEL), lambda i, e: (e, 0)),
                  pl.BlockSpec((te, D_MODEL), lambda i, e: (e, 0))],
        out_specs=pl.BlockSpec((tm, D_MODEL), lambda i, e: (i, 0)),
        out_shape=jax.ShapeDtypeStruct((n, D_MODEL), F32),
        scratch_shapes=[pltpu.VMEM((tm, D_MODEL), F32), pltpu.VMEM((te, tm), F32),
                        pltpu.VMEM((te, tm), BF16)],
        compiler_params=_cparams(("parallel", "arbitrary")),
        name="peer_dense",
    )(x, xn, r2.reshape(bf16_view), bt.reshape(f32_view), e1.reshape(f32_view), e2.reshape(bf16_view), u, vt)


def peer(x, g, wq, keys, u, vt):
    xn, r2, bt, e1, e2 = peer_route(x, g, wq, keys)
    return peer_dense(x, xn, r2, bt, e1, e2, u, vt)


def _pad_cols(w, width):
    return jnp.pad(w, ((0, 0), (0, width - w.shape[1])))


def _rope_tables(pos):
    half = D_ROPE // 2
    inv = jnp.power(ROPE_BASE, -jnp.arange(half, dtype=F32) / half)
    ang = pos.astype(F32)[:, None] * inv[None, :]
    cos2 = jnp.concatenate([jnp.cos(ang), jnp.cos(ang)], axis=1)
    sin2 = jnp.concatenate([jnp.sin(ang), jnp.sin(ang)], axis=1)
    return _pad_cols(cos2, LANES), _pad_cols(sin2, LANES)


def _rot_half_cols(w):
    half = D_ROPE // 2
    return jnp.concatenate([-w[..., half:], w[..., :half]], axis=-1)


def _even_layer_weights(w_in, w_gate, w_out):
    w_in_p = _pad_cols(w_in, 2 * D_A + 2 * D_BK + 2 * D_BV + LANES).astype(BF16)
    wg = jnp.pad(w_gate, ((0, LANES - GATE_RANK), (0, 0))).astype(BF16)
    return w_in_p, wg, w_out.astype(BF16)


def _odd_layer_weights(w_in, w_q_b, w_uk, w_uv, w_out):
    wk = w_in[:, R_Q + R_KV:]
    w_in_p = jnp.concatenate([w_in[:, :R_Q + R_KV], _pad_cols(wk, LANES), _pad_cols(_rot_half_cols(wk), LANES)],
                             axis=1).astype(BF16)
    wq3 = w_q_b.reshape(R_Q, H_C, D_NOPE + D_ROPE)
    wn = wq3[:, :, :D_NOPE].reshape(R_Q, H_C * D_NOPE).astype(BF16)
    wpe = wq3[:, :, D_NOPE:]
    pad3 = lambda w: jnp.pad(w, ((0, 0), (0, 0), (0, LANES - D_ROPE))).reshape(R_Q, H_C * LANES).astype(BF16)
    wa, wb = pad3(wpe), pad3(_rot_half_cols(wpe))
    ukt = jnp.transpose(w_uk, (1, 2, 0))
    uv = jnp.transpose(w_uv, (1, 0, 2))
    odd = (jnp.arange(H_C) % 2 == 1)[:, None, None]
    zk = jnp.zeros_like(ukt)
    wuk = jnp.where(odd, jnp.concatenate([zk, ukt], axis=1), jnp.concatenate([ukt, zk], axis=1)).astype(BF16)
    zv = jnp.zeros_like(uv)
    wuv = jnp.where(odd, jnp.concatenate([zv, uv], axis=2), jnp.concatenate([uv, zv], axis=2)).astype(BF16)
    return w_in_p, wn, wa, wb, wuk, wuv, w_out.astype(BF16)


def kernel(x_prompt, x_sample, cache_ckv, cache_kpe, state_conv, state_gla, page_table, ab_norm, ab_w_in,
           conv_w, conv_b, conv_ln_g, conv_ln_b, gla_w_gate, gla_b_gate, gla_head_g, ab_w_out, c_norm, c_w_in,
           c_q_norm, c_kv_norm, c_w_q_b, c_w_uk, c_w_uv, c_w_out, ffn_norm, peer_w_q, peer_keys, peer_u,
           peer_v, final_norm):
    bp, tp, _ = x_prompt.shape
    bs = x_sample.shape[0]
    n_p = bp * tp
    xp = x_prompt.reshape(n_p, D_MODEL)
    xs = jnp.pad(x_sample.reshape(bs, D_MODEL), ((0, SAMPLE_ROWS - bs), (0, 0)))
    cs_p, sn_p = _rope_tables(jnp.arange(tp, dtype=I32))
    cs_s, sn_s = _rope_tables(jnp.full((SAMPLE_ROWS,), PAST_LEN, I32))
    ckv_p, kpe_p, ckv_s, kpe_s = [], [], [], []
    conv_p, conv_s, gla_p, gla_s = [], [], [], []
    for l in range(DEPTH):
        j = l // 2
        if l % 2 == 0:
            w_in, wg, w_out = _even_layer_weights(ab_w_in[j], gla_w_gate[j], ab_w_out[j])
            widths = (2 * D_A, 2 * D_BK + D_BV, D_BV, LANES)
            za, zqkv, zr, zg = norm_matmul(xp, ab_norm[j], w_in, widths)
            a_out, cbuf = conv_branch_prompt(za, conv_w[j], conv_b[j], conv_ln_g[j], conv_ln_b[j], bp)
            b_out, s_fin = gla_prompt(zqkv, zr, zg, wg, gla_b_gate[j], gla_head_g[j], bp)
            xp = even_out(xp, a_out, b_out, w_out)
            conv_p.append(cbuf)
            gla_p.append(s_fin)
            za, zqkv, zr, zg = norm_matmul(xs, ab_norm[j], w_in, widths)
            st_t = jnp.pad(jnp.transpose(state_conv[j], (1, 0, 2)), ((0, 0), (0, SAMPLE_ROWS - bs), (0, 0)))
            u, a_out, qg, kg, ag = even_sample_a(za, zqkv, zg, st_t, conv_w[j], conv_b[j], conv_ln_g[j],
                                                 conv_ln_b[j], wg, gla_b_gate[j])
            col = lambda t: t[:bs].reshape(bs, D_BK, 1)
            s_new, b_out = even_sample_b(col(ag), col(kg), col(qg),
                                         zqkv[:bs, 2 * D_BK:].reshape(bs, 1, D_BV),
                                         zr[:bs].reshape(bs, 1, D_BV), state_gla[j], gla_head_g[j])
            b_out = jnp.pad(b_out.reshape(bs, D_BV), ((0, SAMPLE_ROWS - bs), (0, 0)))
            xs = even_out(xs, a_out, b_out, w_out)
            conv_s.append(jnp.concatenate([state_conv[j][:, 1:], u[:bs, None, :]], axis=1))
            gla_s.append(s_new)
        else:
            w_in, wn, wa, wb, wuk, wuv, w_out = _odd_layer_weights(c_w_in[j], c_w_q_b[j], c_w_uk[j], c_w_uv[j],
                                                                   c_w_out[j])
            widths = (R_Q, R_KV, LANES, LANES)
            cq, ckv, kpe, kper = norm_matmul(xp, c_norm[j], w_in, widths)
            qc, kc, ckv_o, kpe_o = mla_prep(cq, ckv, kpe, kper, cs_p, sn_p, c_q_norm[j], c_kv_norm[j],
                                            wn, wa, wb, wuk, tp // ATT_TQ)
            ol = mla_attn_prompt(qc, kc, bp)
            xp = mla_out(xp, ol, wuv, w_out)
            ckv_p.append(ckv_o.reshape(bp, tp, R_KV))
            kpe_p.append(kpe_o[:, :D_ROPE].reshape(bp, tp, D_ROPE))
            cq, ckv, kpe, kper = norm_matmul(xs, c_norm[j], w_in, widths)
            qc, kc, ckv_o, kpe_o = mla_prep(cq, ckv, kpe, kper, cs_s, sn_s, c_q_norm[j], c_kv_norm[j],
                                            wn, wa, wb, wuk, 1)
            q_s = jnp.transpose(qc[0, :, :bs, :], (1, 0, 2))
            o_lat = mla_decode(page_table, q_s, kc[:bs].reshape(bs, 1, KC_W), cache_ckv, cache_kpe, j)
            ol = jnp.pad(jnp.transpose(o_lat, (1, 0, 2)), ((0, 0), (0, SAMPLE_ROWS - bs), (0, 0)))
            xs = mla_out(xs, ol.astype(BF16)[None], wuv, w_out)
            ckv_s.append(ckv_o[:bs].reshape(bs, 1, R_KV))
            kpe_s.append(kpe_o[:bs, :D_ROPE].reshape(bs, 1, D_ROPE))
        wq = peer_w_q[l].astype(BF16)
        keys = peer_keys[l].astype(BF16)
        u_tab = peer_u[l].astype(BF16)
        vt_tab = peer_v[l].astype(BF16)
        xp = peer(xp, ffn_norm[l], wq, keys, u_tab, vt_tab)
        xs = peer(xs, ffn_norm[l], wq, keys, u_tab, vt_tab)
    y_prompt = final_rms(xp, final_norm).reshape(bp, tp, D_MODEL)
    y_sample = final_rms(xs, final_norm)[:bs].reshape(bs, 1, D_MODEL)
    return (y_prompt, y_sample,
            jnp.stack(ckv_p), jnp.stack(kpe_p), jnp.stack(ckv_s), jnp.stack(kpe_s),
            jnp.stack(conv_p), jnp.stack(conv_s), jnp.stack(gla_p), jnp.stack(gla_s))
```

```python
import functools

import jax
import jax.numpy as jnp
from jax import lax
from jax.experimental import pallas as pl
from jax.experimental.pallas import tpu as pltpu

F32 = jnp.float32
BF16 = jnp.bfloat16
I32 = jnp.int32

D_MODEL = 1024
SEQ = 2048
DEPTH = 4
DEC_BATCH = 32
PAST_LEN = 16384
PAGE_SIZE = 128
EPS = 1e-6
D_A = 512
CONV_W = 31
H_B = 4
D_BV = 512
D_BK = 256
DK_B = 64
DV_B = 128
GATE_RANK = 16
GATE_TAU = 16.0
GLA_CHUNK = 64
GLA_SUB = 16
H_C = 16
D_NOPE = 64
D_ROPE = 32
D_V = 64
R_Q = 384
R_KV = 256
ROPE_BASE = 10000.0
MLA_SCALE = (D_NOPE + D_ROPE) ** -0.5
H_P = 8
N_KEYS = 128
N_EXPERTS = N_KEYS * N_KEYS
D_HALF = 128
TOPK = 16

LANES = 128
SUBLANES = 8
SAMPLE_ROWS = 128
KC_W = R_KV + LANES
ATT_TQ = 128
ATT_TK = 512
DEC_PAGES = 16
VMEM_LIMIT = 56 * 1024 * 1024
NEG_INF = float("-inf")


def _cparams(sem):
    return pltpu.CompilerParams(dimension_semantics=sem, vmem_limit_bytes=VMEM_LIMIT)


def _dot(a, b):
    return jnp.dot(a, b, preferred_element_type=F32)


def _dot_nt(a, b):
    return lax.dot_general(a, b, (((1,), (1,)), ((), ())), preferred_element_type=F32)


def _dot_tn(a, b):
    return lax.dot_general(a, b, (((0,), (0,)), ((), ())), preferred_element_type=F32)


def _rms(x, g):
    return x * lax.rsqrt(jnp.mean(x * x, axis=-1, keepdims=True) + EPS) * g


def _split3(x):
    hi = x.astype(BF16)
    r1 = x - hi.astype(F32)
    mid = r1.astype(BF16)
    lo = (r1 - mid.astype(F32)).astype(BF16)
    return hi, mid, lo


def _const_spec(shape):
    nd = len(shape)
    return pl.BlockSpec(shape, lambda *_: (0,) * nd)


def _norm_mm_kernel(x_ref, g_ref, w_ref, *out_refs, widths):
    xn = _rms(x_ref[...], g_ref[...])
    y = _dot(xn.astype(BF16), w_ref[...])
    off = 0
    for o_ref, w in zip(out_refs, widths):
        o_ref[...] = y[:, off:off + w]
        off += w


def norm_matmul(x, g, w, widths):
    n, k = x.shape
    tm = min(512, n)
    return pl.pallas_call(
        functools.partial(_norm_mm_kernel, widths=widths),
        grid=(n // tm,),
        in_specs=[pl.BlockSpec((tm, k), lambda i: (i, 0)), _const_spec((1, k)), _const_spec(w.shape)],
        out_specs=[pl.BlockSpec((tm, wd), lambda i: (i, 0)) for wd in widths],
        out_shape=[jax.ShapeDtypeStruct((n, wd), F32) for wd in widths],
        compiler_params=_cparams(("parallel",)),
        name="norm_matmul",
    )(x, g.reshape(1, k), w)


def _final_norm_kernel(x_ref, g_ref, o_ref):
    o_ref[...] = _rms(x_ref[...], g_ref[...])


def final_rms(x, g):
    n, k = x.shape
    tm = min(512, n)
    return pl.pallas_call(
        _final_norm_kernel,
        grid=(n // tm,),
        in_specs=[pl.BlockSpec((tm, k), lambda i: (i, 0)), _const_spec((1, k))],
        out_specs=pl.BlockSpec((tm, k), lambda i: (i, 0)),
        out_shape=jax.ShapeDtypeStruct((n, k), F32),
        compiler_params=_cparams(("parallel",)),
        name="final_norm",
    )(x, g.reshape(1, k))


def _ln_swish(c, g, b):
    mu = jnp.mean(c, axis=-1, keepdims=True)
    xc = c - mu
    y = xc * lax.rsqrt(jnp.mean(xc * xc, axis=-1, keepdims=True) + EPS) * g + b
    return y * jax.nn.sigmoid(y)


CONV_HIST = 32


def _conv_kernel(za_ref, cw_ref, cb_ref, lg_ref, lb_ref, out_ref, buf_ref, ext_ref, *, tt, nt):
    t = pl.program_id(1)

    @pl.when(t == 0)
    def _():
        ext_ref[0:CONV_HIST, :] = jnp.zeros((CONV_HIST, D_A), F32)

    za = za_ref[...]
    ext_ref[CONV_HIST:CONV_HIST + tt, :] = za[:, :D_A] * jax.nn.sigmoid(za[:, D_A:])
    first = CONV_HIST - (CONV_W - 1)
    acc = jnp.zeros((tt, D_A), F32) + cb_ref[...]
    for w in range(CONV_W):
        acc = acc + ext_ref[pl.ds(first + w, tt), :] * cw_ref[w:w + 1, :]
    out_ref[...] = _ln_swish(acc, lg_ref[...], lb_ref[...])

    @pl.when(t == nt - 1)
    def _():
        buf_ref[...] = ext_ref[pl.ds(tt + first, CONV_W - 1), :]

    ext_ref[0:CONV_HIST, :] = ext_ref[tt:tt + CONV_HIST, :]


def conv_branch_prompt(za, cw, cb, lg, lb, batch):
    n = za.shape[0]
    seq = n // batch
    tt = 512
    nt = seq // tt
    return pl.pallas_call(
        functools.partial(_conv_kernel, tt=tt, nt=nt),
        grid=(batch, nt),
        in_specs=[pl.BlockSpec((tt, 2 * D_A), lambda b, t: (b * nt + t, 0)),
                  _const_spec((CONV_W, D_A)), _const_spec((1, D_A)), _const_spec((1, D_A)),
                  _const_spec((1, D_A))],
        out_specs=[pl.BlockSpec((tt, D_A), lambda b, t: (b * nt + t, 0)),
                   pl.BlockSpec((None, CONV_W - 1, D_A), lambda b, t: (b, 0, 0))],
        out_shape=[jax.ShapeDtypeStruct((n, D_A), F32),
                   jax.ShapeDtypeStruct((batch, CONV_W - 1, D_A), F32)],
        scratch_shapes=[pltpu.VMEM((tt + CONV_HIST, D_A), F32)],
        compiler_params=_cparams(("parallel", "arbitrary")),
        name="conv_branch_prompt",
    )(za, cw, cb.reshape(1, D_A), lg.reshape(1, D_A), lb.reshape(1, D_A))


def _log_decay(zg, wg, bg):
    x = _dot(zg.astype(BF16), wg) + bg
    return (jnp.minimum(x, 0.0) - jnp.log1p(jnp.exp(-jnp.abs(x)))) * (1.0 / GATE_TAU)


def _head_norm_gate(o, hg, r):
    outs = []
    for h in range(H_B):
        oh = o[:, h * DV_B:(h + 1) * DV_B]
        outs.append(oh * lax.rsqrt(jnp.mean(oh * oh, axis=-1, keepdims=True) + EPS) * hg)
    return jnp.concatenate(outs, axis=1) * (r * jax.nn.sigmoid(r))


def _gla_kernel(zqkv_ref, zr_ref, zg_ref, wg_ref, bg_ref, hg_ref, ltri_ref, ones_ref,
                bout_ref, sfin_ref, s_ref, *, tt, nt):
    t = pl.program_id(1)

    @pl.when(t == 0)
    def _():
        s_ref[...] = jnp.zeros((D_BK, D_BV), F32)

    zq = zqkv_ref[...]
    q = zq[:, 0:D_BK] * (DK_B ** -0.5)
    k = zq[:, D_BK:2 * D_BK]
    v = zq[:, 2 * D_BK:]
    la = _log_decay(zg_ref[...], wg_ref[...], bg_ref[...])
    la3 = _split3(la)
    ltri = ltri_ref[...]
    b = _dot(ltri, la3[0]) + _dot(ltri, la3[1]) + _dot(ltri, la3[2])
    ones = ones_ref[...]

    lane_head = lax.broadcasted_iota(I32, (GLA_SUB, D_BK), 1) // DK_B
    bd_mask = (lax.broadcasted_iota(I32, (D_BK, D_BV), 0) // DK_B
               == lax.broadcasted_iota(I32, (D_BK, D_BV), 1) // DV_B)
    o_chunks = []
    for c in range(tt // GLA_CHUNK):
        r0 = c * GLA_CHUNK
        bc = b[r0:r0 + GLA_CHUNK]
        qc = q[r0:r0 + GLA_CHUNK]
        kc = k[r0:r0 + GLA_CHUNK]
        vc = v[r0:r0 + GLA_CHUNK]
        vcb = vc.astype(BF16)
        s_prev = s_ref[...]
        o = _dot((qc * jnp.exp(bc)).astype(BF16), s_prev.astype(BF16))
        o_rows = []
        for i in range(GLA_CHUNK // GLA_SUB):
            i0 = i * GLA_SUB
            nk = i0 + GLA_SUB
            ref_b = bc[i0:i0 + 1, :]
            qi = qc[i0:nk] * jnp.exp(bc[i0:nk] - ref_b)
            ki = kc[0:nk] * jnp.exp(ref_b - bc[0:nk])
            qs = jnp.concatenate([jnp.where(lane_head == h, qi, 0.0) for h in range(H_B)], axis=0)
            att = _dot_nt(qs.astype(BF16), ki.astype(BF16))
            row_i = lax.broadcasted_iota(I32, (H_B * GLA_SUB, nk), 0) % GLA_SUB
            col_j = lax.broadcasted_iota(I32, (H_B * GLA_SUB, nk), 1)
            att = jnp.where(col_j <= row_i + i0, att, 0.0)
            res = _dot(att.astype(BF16), vcb[0:nk])
            o_rows.append(jnp.concatenate(
                [res[h * GLA_SUB:(h + 1) * GLA_SUB, h * DV_B:(h + 1) * DV_B] for h in range(H_B)], axis=1))
        o_chunks.append(o + jnp.concatenate(o_rows, axis=0))
        b_end = bc[GLA_CHUNK - 1:GLA_CHUNK, :]
        kst = kc * jnp.exp(b_end - bc)
        kv = _dot_tn(kst.astype(BF16), vcb)
        b_end_t = (_dot_tn(la3[0][r0:r0 + GLA_CHUNK], ones) + _dot_tn(la3[1][r0:r0 + GLA_CHUNK], ones)
                   + _dot_tn(la3[2][r0:r0 + GLA_CHUNK], ones))
        dec = jnp.exp(b_end_t)
        s_ref[...] = s_prev * jnp.concatenate([dec] * H_B, axis=1) + jnp.where(bd_mask, kv, 0.0)

    o_all = jnp.concatenate(o_chunks, axis=0)
    bout_ref[...] = _head_norm_gate(o_all, hg_ref[...], zr_ref[...])

    @pl.when(t == nt - 1)
    def _():
        for h in range(H_B):
            sfin_ref[h] = s_ref[h * DK_B:(h + 1) * DK_B, h * DV_B:(h + 1) * DV_B]


def gla_prompt(zqkv, zr, zg, wg, bg, hg, batch):
    n = zqkv.shape[0]
    seq = n // batch
    tt = 256
    nt = seq // tt
    ri = lax.broadcasted_iota(I32, (tt, tt), 0)
    ci = lax.broadcasted_iota(I32, (tt, tt), 1)
    ltri = ((ri // GLA_CHUNK == ci // GLA_CHUNK) & (ci <= ri)).astype(BF16)
    ones = jnp.ones((GLA_CHUNK, LANES), BF16)
    row = lambda b, t: (b * nt + t, 0)
    return pl.pallas_call(
        functools.partial(_gla_kernel, tt=tt, nt=nt),
        grid=(batch, nt),
        in_specs=[pl.BlockSpec((tt, 2 * D_BK + D_BV), row), pl.BlockSpec((tt, D_BV), row),
                  pl.BlockSpec((tt, LANES), row), _const_spec((LANES, D_BK)), _const_spec((1, D_BK)),
                  _const_spec((1, DV_B)), _const_spec((tt, tt)), _const_spec((GLA_CHUNK, LANES))],
        out_specs=[pl.BlockSpec((tt, D_BV), row),
                   pl.BlockSpec((None, H_B, DK_B, DV_B), lambda b, t: (b, 0, 0, 0))],
        out_shape=[jax.ShapeDtypeStruct((n, D_BV), F32),
                   jax.ShapeDtypeStruct((batch, H_B, DK_B, DV_B), F32)],
        scratch_shapes=[pltpu.VMEM((D_BK, D_BV), F32)],
        compiler_params=_cparams(("parallel", "arbitrary")),
        name="gla_prompt",
    )(zqkv, zr, zg, wg, bg.reshape(1, D_BK), hg.reshape(1, DV_B), ltri, ones)


def _even_sample_a_kernel(za_ref, zqkv_ref, zg_ref, st_ref, cw_ref, cb_ref, lg_ref, lb_ref, wg_ref, bg_ref,
                          u_ref, aout_ref, q_ref, k_ref, a_ref):
    za = za_ref[...]
    u = za[:, :D_A] * jax.nn.sigmoid(za[:, D_A:])
    u_ref[...] = u
    acc = jnp.zeros_like(u) + cb_ref[...]
    for w in range(CONV_W - 1):
        acc = acc + st_ref[w] * cw_ref[w:w + 1, :]
    acc = acc + u * cw_ref[CONV_W - 1:CONV_W, :]
    aout_ref[...] = _ln_swish(acc, lg_ref[...], lb_ref[...])
    zq = zqkv_ref[...]
    q_ref[...] = zq[:, 0:D_BK] * (DK_B ** -0.5)
    k_ref[...] = zq[:, D_BK:2 * D_BK]
    a_ref[...] = jnp.exp(_log_decay(zg_ref[...], wg_ref[...], bg_ref[...]))


def even_sample_a(za, zqkv, zg, st_t, cw, cb, lg, lb, wg, bg):
    n = za.shape[0]
    shp = lambda w: jax.ShapeDtypeStruct((n, w), F32)
    return pl.pallas_call(
        _even_sample_a_kernel,
        out_shape=[shp(D_A), shp(D_A), shp(D_BK), shp(D_BK), shp(D_BK)],
        compiler_params=pltpu.CompilerParams(vmem_limit_bytes=VMEM_LIMIT),
        name="even_sample_a",
    )(za, zqkv, zg, st_t, cw, cb.reshape(1, D_A), lg.reshape(1, D_A), lb.reshape(1, D_A), wg,
      bg.reshape(1, D_BK))


def _even_sample_b_kernel(a_ref, k_ref, q_ref, v_ref, r_ref, s_ref, hg_ref, snew_ref, bout_ref):
    v = v_ref[...]
    outs = []
    for h in range(H_B):
        rows = slice(h * DK_B, (h + 1) * DK_B)
        vh = v[:, h * DV_B:(h + 1) * DV_B]
        s_new = a_ref[rows, :] * s_ref[h] + k_ref[rows, :] * vh
        snew_ref[h] = s_new
        outs.append(jnp.sum(q_ref[rows, :] * s_new, axis=0, keepdims=True))
    bout_ref[...] = _head_norm_gate(jnp.concatenate(outs, axis=1), hg_ref[...], r_ref[...])


def even_sample_b(a3, k3, q3, v3, r3, state, hg):
    nb = a3.shape[0]
    col = pl.BlockSpec((None, D_BK, 1), lambda b: (b, 0, 0))
    rowv = pl.BlockSpec((None, 1, D_BV), lambda b: (b, 0, 0))
    st = pl.BlockSpec((None, H_B, DK_B, DV_B), lambda b: (b, 0, 0, 0))
    return pl.pallas_call(
        _even_sample_b_kernel,
        grid=(nb,),
        in_specs=[col, col, col, rowv, rowv, st, _const_spec((1, DV_B))],
        out_specs=[st, rowv],
        out_shape=[jax.ShapeDtypeStruct((nb, H_B, DK_B, DV_B), F32),
                   jax.ShapeDtypeStruct((nb, 1, D_BV), F32)],
        compiler_params=_cparams(("parallel",)),
        name="even_sample_b",
    )(a3, k3, q3, v3, r3, state, hg.reshape(1, DV_B))


def _even_out_kernel(x_ref, a_ref, b_ref, w_ref, o_ref):
    y = _dot(a_ref[...].astype(BF16), w_ref[0:D_A, :]) + _dot(b_ref[...].astype(BF16), w_ref[D_A:, :])
    o_ref[...] = x_ref[...] + y


def even_out(x, a, b, w):
    n = x.shape[0]
    tm = min(512, n)
    row = lambda i: (i, 0)
    return pl.pallas_call(
        _even_out_kernel,
        grid=(n // tm,),
        in_specs=[pl.BlockSpec((tm, D_MODEL), row), pl.BlockSpec((tm, D_A), row),
                  pl.BlockSpec((tm, D_BV), row), _const_spec(w.shape)],
        out_specs=pl.BlockSpec((tm, D_MODEL), row),
        out_shape=jax.ShapeDtypeStruct((n, D_MODEL), F32),
        compiler_params=_cparams(("parallel",)),
        name="even_out",
    )(x, a, b, w)


def _mla_prep_kernel(cq_ref, ckv_ref, kpe_ref, kper_ref, cs_ref, sn_ref, qg_ref, kvg_ref,
                     wn_ref, wa_ref, wb_ref, wuk_ref, qc_ref, kc_ref, ckvo_ref, kpeo_ref):
    cs = cs_ref[...]
    sn = sn_ref[...]
    ckv = _rms(ckv_ref[...], kvg_ref[...])
    kpe = kpe_ref[...] * cs + kper_ref[...] * sn
    ckvo_ref[...] = ckv
    kpeo_ref[...] = kpe
    kc_ref[:, 0:R_KV] = ckv.astype(BF16)
    kc_ref[:, R_KV:] = kpe.astype(BF16)
    cqn = _rms(cq_ref[...], qg_ref[...]).astype(BF16)
    qn = _dot(cqn, wn_ref[...]).astype(BF16)
    qa = _dot(cqn, wa_ref[...])
    qb = _dot(cqn, wb_ref[...])
    for h in range(H_C):
        pair = qn[:, (h // 2) * LANES:(h // 2 + 1) * LANES]
        ql = (_dot(pair, wuk_ref[h]) * MLA_SCALE).astype(BF16)
        qp = ((qa[:, h * LANES:(h + 1) * LANES] * cs + qb[:, h * LANES:(h + 1) * LANES] * sn)
              * MLA_SCALE).astype(BF16)
        for t in range(qc_ref.shape[0]):
            tok = slice(t * ATT_TQ, (t + 1) * ATT_TQ)
            qc_ref[t, h, :, 0:R_KV] = ql[tok]
            qc_ref[t, h, :, R_KV:] = qp[tok]


def mla_prep(cq, ckv, kpe, kper, cs, sn, qg, kvg, wn, wa, wb, wuk, pos_tiles):
    n = cq.shape[0]
    tm = min(512, n)
    sub = tm // ATT_TQ
    row = lambda i: (i, 0)
    pos = lambda i: (i % (pos_tiles // sub), 0)
    return pl.pallas_call(
        _mla_prep_kernel,
        grid=(n // tm,),
        in_specs=[pl.BlockSpec((tm, R_Q), row), pl.BlockSpec((tm, R_KV), row),
                  pl.BlockSpec((tm, LANES), row), pl.BlockSpec((tm, LANES), row),
                  pl.BlockSpec((tm, LANES), pos), pl.BlockSpec((tm, LANES), pos),
                  _const_spec((1, R_Q)), _const_spec((1, R_KV)),
                  _const_spec(wn.shape), _const_spec(wa.shape), _const_spec(wb.shape), _const_spec(wuk.shape)],
        out_specs=[pl.BlockSpec((sub, H_C, ATT_TQ, KC_W), lambda i: (i, 0, 0, 0)),
                   pl.BlockSpec((tm, KC_W), row), pl.BlockSpec((tm, R_KV), row),
                   pl.BlockSpec((tm, LANES), row)],
        out_shape=[jax.ShapeDtypeStruct((n // ATT_TQ, H_C, ATT_TQ, KC_W), BF16),
                   jax.ShapeDtypeStruct((n, KC_W), BF16),
                   jax.ShapeDtypeStruct((n, R_KV), F32),
                   jax.ShapeDtypeStruct((n, LANES), F32)],
        compiler_params=_cparams(("parallel",)),
        name="mla_prep",
    )(cq, ckv, kpe, kper, cs, sn, qg.reshape(1, R_Q), kvg.reshape(1, R_KV), wn, wa, wb, wuk)


ATT_STRIP = H_C * ATT_TQ


def _attn_strip_update(q, kc, vt, visible, carry, acc_ref):
    m_old, l_old = carry
    st = _dot_nt(kc, q)
    if visible is not None:
        st = jnp.where(visible, st, NEG_INF)
    m_new = jnp.maximum(m_old, jnp.max(st, axis=0, keepdims=True))
    p = jnp.exp(st - m_new)
    alpha = jnp.exp(m_old - m_new)
    acc_ref[...] = alpha * acc_ref[...] + _dot(vt, p.astype(BF16))
    return m_new, alpha * l_old + jnp.sum(p, axis=0, keepdims=True)


def _mla_attn_kernel(qc_ref, kc_ref, vt_ref, ol_ref, acc_ref):
    i = pl.program_id(1)
    n_full = (i * ATT_TQ) // ATT_TK
    k0 = pl.multiple_of(n_full * ATT_TK, ATT_TK)
    tok = i * ATT_TQ + lax.broadcasted_iota(I32, (ATT_TK, ATT_STRIP), 1) % ATT_TQ
    key = k0 + lax.broadcasted_iota(I32, (ATT_TK, ATT_STRIP), 0)
    causal = key <= tok
    heads = ATT_STRIP // ATT_TQ
    n_strip = H_C // heads
    qs = [qc_ref[heads * s:heads * (s + 1)].reshape(ATT_STRIP, KC_W) for s in range(n_strip)]
    acc_ref[...] = jnp.zeros((n_strip, R_KV, ATT_STRIP), F32)

    def full_block(j, carry):
        kc = kc_ref[pl.ds(pl.multiple_of(j * ATT_TK, ATT_TK), ATT_TK), :]
        vt = vt_ref[j]
        return tuple(_attn_strip_update(qs[s], kc, vt, None, carry[s], acc_ref.at[s]) for s in range(n_strip))

    init = (jnp.full((1, ATT_STRIP), NEG_INF, F32), jnp.zeros((1, ATT_STRIP), F32))
    carry = lax.fori_loop(0, n_full, full_block, (init,) * n_strip)
    kc = kc_ref[pl.ds(k0, ATT_TK), :]
    vt = vt_ref[n_full]
    for s in range(n_strip):
        _, l_fin = _attn_strip_update(qs[s], kc, vt, causal, carry[s], acc_ref.at[s])
        o = (acc_ref[s] / l_fin).T
        ol_ref[heads * s:heads * (s + 1)] = o.reshape(heads, ATT_TQ, R_KV).astype(BF16)


def mla_attn_prompt(qc, kc, batch):
    ntile = qc.shape[0]
    nq = ntile // batch
    seq = nq * ATT_TQ
    nk = seq // ATT_TK
    vt = jnp.transpose(kc[:, :R_KV].reshape(batch, nk, ATT_TK, R_KV), (0, 1, 3, 2))
    return pl.pallas_call(
        _mla_attn_kernel,
        grid=(batch, nq),
        in_specs=[pl.BlockSpec((None, H_C, ATT_TQ, KC_W), lambda b, i: (b * nq + i, 0, 0, 0)),
                  pl.BlockSpec((seq, KC_W), lambda b, i: (b, 0)),
                  pl.BlockSpec((None, nk, R_KV, ATT_TK), lambda b, i: (b, 0, 0, 0))],
        out_specs=pl.BlockSpec((None, H_C, ATT_TQ, R_KV), lambda b, i: (b * nq + i, 0, 0, 0)),
        out_shape=jax.ShapeDtypeStruct((ntile, H_C, ATT_TQ, R_KV), BF16),
        scratch_shapes=[pltpu.VMEM((H_C * ATT_TQ // ATT_STRIP, R_KV, ATT_STRIP), F32)],
        compiler_params=_cparams(("parallel", "arbitrary")),
        name="mla_attn_prompt",
    )(qc, kc, vt)


def _mla_decode_kernel(pt_ref, q_ref, kcur_ref, *refs, ng):
    ckv_refs = refs[:DEC_PAGES]
    kpe_refs = refs[DEC_PAGES:2 * DEC_PAGES]
    o_ref, m_ref, l_ref, acc_ref = refs[2 * DEC_PAGES:]
    g = pl.program_id(1)

    @pl.when(g == 0)
    def _():
        m_ref[...] = jnp.full((H_C, 1), NEG_INF, F32)
        l_ref[...] = jnp.zeros((H_C, 1), F32)
        acc_ref[...] = jnp.zeros((H_C, R_KV), F32)

    q = q_ref[...]
    ql = q[:, 0:R_KV]
    qp = q[:, R_KV:R_KV + D_ROPE]
    cks = [r[...].astype(BF16) for r in ckv_refs]
    s = jnp.concatenate(
        [_dot_nt(ql, cks[p]) + _dot_nt(qp, kpe_refs[p][...].astype(BF16)) for p in range(DEC_PAGES)], axis=1)
    m_old = m_ref[...]
    m_new = jnp.maximum(m_old, jnp.max(s, axis=1, keepdims=True))
    p_all = jnp.exp(s - m_new).astype(BF16)
    alpha = jnp.exp(m_old - m_new)
    pv = _dot(p_all[:, 0:PAGE_SIZE], cks[0])
    for p in range(1, DEC_PAGES):
        pv = pv + _dot(p_all[:, p * PAGE_SIZE:(p + 1) * PAGE_SIZE], cks[p])
    l_new = alpha * l_ref[...] + jnp.sum(p_all.astype(F32), axis=1, keepdims=True)
    acc_new = alpha * acc_ref[...] + pv
    m_ref[...] = m_new
    l_ref[...] = l_new
    acc_ref[...] = acc_new

    @pl.when(g == ng - 1)
    def _():
        kcur = kcur_ref[...].astype(F32)
        s_cur = jnp.sum(q.astype(F32) * kcur, axis=1, keepdims=True)
        m_fin = jnp.maximum(m_new, s_cur)
        a_fin = jnp.exp(m_new - m_fin)
        p_cur = jnp.exp(s_cur - m_fin)
        l_fin = a_fin * l_new + p_cur
        o_ref[...] = (a_fin * acc_new + p_cur * kcur[:, 0:R_KV]) / l_fin


def mla_decode(page_table, q_s, kc_s, cache_ckv, cache_kpe, layer):
    nb, npages = page_table.shape
    ng = npages // DEC_PAGES

    def page_spec(width, p):
        return pl.BlockSpec((None, None, PAGE_SIZE, width),
                            lambda b, g, pt: (layer, pt[b, g * DEC_PAGES + p], 0, 0))

    grid_spec = pltpu.PrefetchScalarGridSpec(
        num_scalar_prefetch=1,
        grid=(nb, ng),
        in_specs=([pl.BlockSpec((None, H_C, KC_W), lambda b, g, pt: (b, 0, 0)),
                   pl.BlockSpec((None, 1, KC_W), lambda b, g, pt: (b, 0, 0))]
                  + [page_spec(R_KV, p) for p in range(DEC_PAGES)]
                  + [page_spec(D_ROPE, p) for p in range(DEC_PAGES)]),
        out_specs=pl.BlockSpec((None, H_C, R_KV), lambda b, g, pt: (b, 0, 0)),
        scratch_shapes=[pltpu.VMEM((H_C, 1), F32), pltpu.VMEM((H_C, 1), F32), pltpu.VMEM((H_C, R_KV), F32)],
    )
    return pl.pallas_call(
        functools.partial(_mla_decode_kernel, ng=ng),
        grid_spec=grid_spec,
        out_shape=jax.ShapeDtypeStruct((nb, H_C, R_KV), F32),
        compiler_params=_cparams(("parallel", "arbitrary")),
        name="mla_decode",
    )(page_table, q_s, kc_s, *([cache_ckv] * DEC_PAGES), *([cache_kpe] * DEC_PAGES))


def _mla_out_kernel(x_ref, ol_ref, wuv_ref, wo_ref, o_ref):
    rows = x_ref.shape[0]
    head = lambda h: ol_ref[:, h].reshape(rows, R_KV)
    pairs = []
    for p in range(H_C // 2):
        pairs.append((_dot(head(2 * p), wuv_ref[2 * p]) + _dot(head(2 * p + 1), wuv_ref[2 * p + 1])).astype(BF16))
    o = jnp.concatenate(pairs, axis=1)
    o_ref[...] = x_ref[...] + _dot(o, wo_ref[...])


def mla_out(x, ol, wuv, wo):
    n = x.shape[0]
    tm = min(512, n)
    return pl.pallas_call(
        _mla_out_kernel,
        grid=(n // tm,),
        in_specs=[pl.BlockSpec((tm, D_MODEL), lambda i: (i, 0)),
                  pl.BlockSpec((tm // ATT_TQ, H_C, ATT_TQ, R_KV), lambda i: (i, 0, 0, 0)),
                  _const_spec(wuv.shape), _const_spec(wo.shape)],
        out_specs=pl.BlockSpec((tm, D_MODEL), lambda i: (i, 0)),
        out_shape=jax.ShapeDtypeStruct((n, D_MODEL), F32),
        compiler_params=_cparams(("parallel",)),
        name="mla_out",
    )(x, ol, wuv, wo)


def _top16_rows(s):
    rows, tm = s.shape
    iota = lax.broadcasted_iota(I32, (rows, tm), 0)
    rank = jnp.full((rows, tm), TOPK, I32)
    vals = []
    cur = s
    for it in range(TOPK):
        m = jnp.max(cur, axis=0, keepdims=True)
        idx = jnp.min(jnp.where(cur == m, iota, rows), axis=0, keepdims=True)
        sel = iota == idx
        rank = jnp.where(sel, it, rank)
        cur = jnp.where(sel, NEG_INF, cur)
        vals.append(m)
    return jnp.concatenate(vals, axis=0), rank


CAND_ROWS = 80


def _pair_top16(v1, v2):
    cand, _, flat = _pair_candidates(v1, v2)
    sums, a_sel = [], []
    for _ in range(TOPK):
        m = jnp.max(cand, axis=0, keepdims=True)
        ci = jnp.min(jnp.where(cand == m, flat, TOPK * TOPK), axis=0, keepdims=True)
        cand = jnp.where(flat == ci, NEG_INF, cand)
        sums.append(m)
        a_sel.append(ci >> 4)
    return sums, a_sel


def _top16_rows_distinct(s):
    rows, tm = s.shape
    rank = jnp.full((rows, tm), float(TOPK), F32)
    vals = []
    cur = s
    for it in range(TOPK):
        m = jnp.max(cur, axis=0, keepdims=True)
        sel = cur == m
        rank = jnp.where(sel, float(it), rank)
        cur = jnp.where(sel, NEG_INF, cur)
        vals.append(m)
    marked = jnp.sum(jnp.where(rank < float(TOPK), 1.0, 0.0), axis=0, keepdims=True)
    return jnp.concatenate(vals, axis=0), rank, marked


def _pair_candidates(v1, v2):
    tm = v1.shape[1]
    pieces = [v1[0:1] + v2]
    for a in range(1, 8):
        pieces.append(v1[a:a + 1] + v2[0:8])
    pieces.append(v1[8:16] + v2[0:1])
    cand = jnp.concatenate(pieces, axis=0)
    r = lax.broadcasted_iota(I32, (CAND_ROWS, tm), 0)
    a_idx = jnp.where(r < 16, 0, jnp.where(r < 72, 1 + ((r - 16) >> 3), r - 64))
    b_idx = jnp.where(r < 16, r, jnp.where(r < 72, (r - 16) & 7, 0))
    cand = jnp.where((a_idx + 1) * (b_idx + 1) <= TOPK, cand, NEG_INF)
    return cand, a_idx, a_idx * TOPK + b_idx


def _pair_top16_distinct(v1, v2):
    cand, _, _ = _pair_candidates(v1, v2)
    cur = cand
    sums = []
    for _ in range(TOPK):
        m = jnp.max(cur, axis=0, keepdims=True)
        cur = jnp.where(cur == m, NEG_INF, cur)
        sums.append(m)
    picked = jnp.where((cur == NEG_INF) & (cand > NEG_INF), 1.0, 0.0)
    counts = [jnp.sum(picked[0:16], axis=0, keepdims=True)]
    for a in range(1, 8):
        counts.append(jnp.sum(picked[8 + 8 * a:16 + 8 * a], axis=0, keepdims=True))
    cnt = jnp.concatenate(counts + [picked[72:80]], axis=0)
    return sums, cnt, jnp.sum(picked, axis=0, keepdims=True)


def _bf16_twice(x):
    hi = pltpu.bitcast(x.astype(BF16).astype(F32), jnp.uint32)
    return hi | (hi >> 16)


def _route_tables(s1, s2, rank1, rank2, v1, v2, sums, cnt, refs, h):
    r2_ref, bt_ref, e1_ref, e2_ref = refs
    z = jnp.exp(sums[0] - sums[0])
    for kk in range(1, TOPK):
        z = z + jnp.exp(sums[kk] - sums[0])
    bt = jnp.zeros(s1.shape, F32)
    for a in range(TOPK):
        bt = jnp.where(rank1 == a, cnt[a:a + 1], bt)
    e1 = jnp.exp(s1 - v1[0:1]) * (1.0 / z)
    e2 = jnp.exp(s2 - v2[0:1])
    r2 = rank2.astype(F32)
    for c in range(s1.shape[1] // LANES):
        cols = slice(c * LANES, (c + 1) * LANES)
        bt_ref[h, c] = _bf16_twice(bt[:, cols])
        e1_ref[h, c] = _bf16_twice(e1[:, cols])
        e2_ref[h, c] = pltpu.bitcast(e2[:, cols].astype(BF16), jnp.uint32)
        r2_ref[h, c] = pltpu.bitcast(r2[:, cols].astype(BF16), jnp.uint32)


def _peer_route_kernel(x_ref, g_ref, wq_ref, keys_ref, xn_ref, r2_ref, bt_ref, e1_ref, e2_ref, q_ref, bad_ref):
    xn = _rms(x_ref[...], g_ref[...]).astype(BF16)
    xn_ref[...] = xn
    q = _dot(xn, wq_ref[...]).astype(BF16)
    tm = q.shape[0]
    for k in range(2 * H_P):
        q_ref[k] = q[:, k * D_HALF:(k + 1) * D_HALF]
    refs = (r2_ref, bt_ref, e1_ref, e2_ref)

    def scores(h):
        return _dot_nt(keys_ref[h, 0], q_ref[2 * h]), _dot_nt(keys_ref[h, 1], q_ref[2 * h + 1])

    for h in range(H_P):
        s1, s2 = scores(h)
        v1, rank1, n1 = _top16_rows_distinct(s1)
        v2, rank2, n2 = _top16_rows_distinct(s2)
        sums, cnt, n3 = _pair_top16_distinct(v1, v2)
        _route_tables(s1, s2, rank1, rank2, v1, v2, sums, cnt, refs, h)
        bad_ref[h:h + 1, :] = jnp.abs(n1 - TOPK) + jnp.abs(n2 - TOPK) + jnp.abs(n3 - TOPK)

    def redo(h, carry):
        @pl.when(jnp.max(bad_ref[pl.ds(h, 1), :]) > 0.0)
        def _():
            iota16 = lax.broadcasted_iota(I32, (TOPK, tm), 0)
            s1, s2 = scores(h)
            v1, rank1 = _top16_rows(s1)
            v2, rank2 = _top16_rows(s2)
            sums, a_sel = _pair_top16(v1, v2)
            cnt = jnp.zeros((TOPK, tm), F32)
            for kk in range(TOPK):
                cnt = cnt + jnp.where(iota16 == a_sel[kk], 1.0, 0.0)
            _route_tables(s1, s2, rank1, rank2, v1, v2, sums, cnt, refs, h)
        return carry

    lax.fori_loop(0, H_P, redo, 0)


def peer_route(x, g, wq, keys):
    n = x.shape[0]
    tm = min(256, n)
    tab = lambda rows: pl.BlockSpec((H_P, tm // LANES, rows, LANES), lambda i: (0, i, 0, 0))
    u32 = jnp.uint32
    tab_shape = lambda rows, dt: jax.ShapeDtypeStruct((H_P, n // LANES, rows, LANES), dt)
    return pl.pallas_call(
        _peer_route_kernel,
        grid=(n // tm,),
        in_specs=[pl.BlockSpec((tm, D_MODEL), lambda i: (i, 0)), _const_spec((1, D_MODEL)),
                  _const_spec(wq.shape), _const_spec(keys.shape)],
        out_specs=[pl.BlockSpec((tm, D_MODEL), lambda i: (i, 0)), tab(N_KEYS // 2), tab(N_KEYS), tab(N_KEYS),
                   tab(N_KEYS // 2)],
        out_shape=[jax.ShapeDtypeStruct((n, D_MODEL), BF16), tab_shape(N_KEYS // 2, u32),
                   tab_shape(N_KEYS, u32), tab_shape(N_KEYS, u32), tab_shape(N_KEYS // 2, u32)],
        scratch_shapes=[pltpu.VMEM((2 * H_P, tm, D_HALF), BF16), pltpu.VMEM((H_P, tm), F32)],
        compiler_params=_cparams(("parallel",)),
        name="peer_route",
    )(x, g.reshape(1, D_MODEL), wq, keys)


def _peer_dense_kernel(x_ref, xn_ref, r2_ref, bt_ref, e1_ref, e2_ref, u_ref, v_ref, o_ref,
                       ht_ref, act_ref, *, tm, te, ne):
    e = pl.program_id(1)

    @pl.when(e == 0)
    def _():
        o_ref[...] = x_ref[...]

    assert te == SUBLANES * N_KEYS
    xn = xn_ref[...]

    def project(pair):
        rows = slice(2 * pair * N_KEYS, (2 * pair + 2) * N_KEYS)
        ht_ref[rows, :] = _dot_nt(u_ref[rows, :], xn)

    groups = N_KEYS // (2 * SUBLANES)

    def token_row(ref, h, c, ii):
        word = jnp.broadcast_to(ref[h, c, e, ii:ii + 1, :], (SUBLANES, LANES))
        return pltpu.bitcast(word, BF16)[None]

    project(0)
    for pair in range(SUBLANES // 2):
        iis = (2 * pair, 2 * pair + 1)
        if pair + 1 < SUBLANES // 2:
            project(pair + 1)
        for c in range(tm // LANES):
            cols = slice(c * LANES, (c + 1) * LANES)
            gates = [jnp.zeros((groups, 2 * SUBLANES, LANES), BF16) for _ in iis]
            for h in range(H_P):
                r2 = pltpu.bitcast(r2_ref[h, c], BF16).reshape(groups, 2 * SUBLANES, LANES)
                e2 = pltpu.bitcast(e2_ref[h, c], BF16).reshape(groups, 2 * SUBLANES, LANES)
                for k, ii in enumerate(iis):
                    val = e2 * token_row(e1_ref, h, c, ii)
                    gates[k] = gates[k] + jnp.where(r2 < token_row(bt_ref, h, c, ii), val, jnp.zeros_like(val))
            for k, ii in enumerate(iis):
                rows = slice(ii * N_KEYS, (ii + 1) * N_KEYS)
                ht = ht_ref[rows, cols]
                gelu = (0.5 * ht * (1.0 + lax.erf(ht * (2.0 ** -0.5)))).astype(BF16)
                act_ref[rows, cols] = gelu * gates[k].reshape(N_KEYS, LANES)
    o_ref[...] += _dot_tn(act_ref[...], v_ref[...])


def peer_dense(x, xn, r2, bt, e1, e2, u, vt):
    n = x.shape[0]
    tm = min(1024, n)
    te = SUBLANES * N_KEYS
    ne = N_EXPERTS // te
    once = dict(pipeline_mode=pl.Buffered(1))
    f32_view = (H_P, n // LANES, N_KEYS // SUBLANES, SUBLANES, LANES)
    bf16_view = (H_P, n // LANES, N_KEYS // 2, LANES)
    tab = lambda view: pl.BlockSpec((H_P, tm // LANES) + view[2:], lambda i, e: (0, i) + (0,) * (len(view) - 2),
                                    **once)
    return pl.pallas_call(
        functools.partial(_peer_dense_kernel, tm=tm, te=te, ne=ne),
        grid=(n // tm, ne),
        in_specs=[pl.BlockSpec((tm, D_MODEL), lambda i, e: (i, 0), **once),
                  pl.BlockSpec((tm, D_MODEL), lambda i, e: (i, 0), **once),
                  tab(bf16_view), tab(f32_view), tab(f32_view), tab(bf16_view),
                  pl.BlockSpec((te, D_MODEL), lambda i, e: (e, 0)),
                  pl.BlockSpec((te, D_MODEL), lambda i, e: (e, 0))],
        out_specs=pl.BlockSpec((tm, D_MODEL), lambda i, e: (i, 0)),
        out_shape=jax.ShapeDtypeStruct((n, D_MODEL), F32),
        scratch_shapes=[pltpu.VMEM((te, tm), F32), pltpu.VMEM((te, tm), BF16)],
        compiler_params=_cparams(("parallel", "arbitrary")),
        name="peer_dense",
    )(x, xn, r2.reshape(bf16_view), bt.reshape(f32_view), e1.reshape(f32_view), e2.reshape(bf16_view), u, vt)


def peer(x, g, wq, keys, u, vt):
    xn, r2, bt, e1, e2 = peer_route(x, g, wq, keys)
    return peer_dense(x, xn, r2, bt, e1, e2, u, vt)


def _pad_cols(w, width):
    return jnp.pad(w, ((0, 0), (0, width - w.shape[1])))


def _rope_tables(pos):
    half = D_ROPE // 2
    inv = jnp.power(ROPE_BASE, -jnp.arange(half, dtype=F32) / half)
    ang = pos.astype(F32)[:, None] * inv[None, :]
    cos2 = jnp.concatenate([jnp.cos(ang), jnp.cos(ang)], axis=1)
    sin2 = jnp.concatenate([jnp.sin(ang), jnp.sin(ang)], axis=1)
    return _pad_cols(cos2, LANES), _pad_cols(sin2, LANES)


def _rot_half_cols(w):
    half = D_ROPE // 2
    return jnp.concatenate([-w[..., half:], w[..., :half]], axis=-1)


def _even_layer_weights(w_in, w_gate, w_out):
    w_in_p = _pad_cols(w_in, 2 * D_A + 2 * D_BK + 2 * D_BV + LANES).astype(BF16)
    wg = jnp.pad(w_gate, ((0, LANES - GATE_RANK), (0, 0))).astype(BF16)
    return w_in_p, wg, w_out.astype(BF16)


def _odd_layer_weights(w_in, w_q_b, w_uk, w_uv, w_out):
    wk = w_in[:, R_Q + R_KV:]
    w_in_p = jnp.concatenate([w_in[:, :R_Q + R_KV], _pad_cols(wk, LANES), _pad_cols(_rot_half_cols(wk), LANES)],
                             axis=1).astype(BF16)
    wq3 = w_q_b.reshape(R_Q, H_C, D_NOPE + D_ROPE)
    wn = wq3[:, :, :D_NOPE].reshape(R_Q, H_C * D_NOPE).astype(BF16)
    wpe = wq3[:, :, D_NOPE:]
    pad3 = lambda w: jnp.pad(w, ((0, 0), (0, 0), (0, LANES - D_ROPE))).reshape(R_Q, H_C * LANES).astype(BF16)
    wa, wb = pad3(wpe), pad3(_rot_half_cols(wpe))
    ukt = jnp.transpose(w_uk, (1, 2, 0))
    uv = jnp.transpose(w_uv, (1, 0, 2))
    odd = (jnp.arange(H_C) % 2 == 1)[:, None, None]
    zk = jnp.zeros_like(ukt)
    wuk = jnp.where(odd, jnp.concatenate([zk, ukt], axis=1), jnp.concatenate([ukt, zk], axis=1)).astype(BF16)
    zv = jnp.zeros_like(uv)
    wuv = jnp.where(odd, jnp.concatenate([zv, uv], axis=2), jnp.concatenate([uv, zv], axis=2)).astype(BF16)
    return w_in_p, wn, wa, wb, wuk, wuv, w_out.astype(BF16)


def kernel(x_prompt, x_sample, cache_ckv, cache_kpe, state_conv, state_gla, page_table, ab_norm, ab_w_in,
           conv_w, conv_b, conv_ln_g, conv_ln_b, gla_w_gate, gla_b_gate, gla_head_g, ab_w_out, c_norm, c_w_in,
           c_q_norm, c_kv_norm, c_w_q_b, c_w_uk, c_w_uv, c_w_out, ffn_norm, peer_w_q, peer_keys, peer_u,
           peer_v, final_norm):
    bp, tp, _ = x_prompt.shape
    bs = x_sample.shape[0]
    n_p = bp * tp
    xp = x_prompt.reshape(n_p, D_MODEL)
    xs = jnp.pad(x_sample.reshape(bs, D_MODEL), ((0, SAMPLE_ROWS - bs), (0, 0)))
    cs_p, sn_p = _rope_tables(jnp.arange(tp, dtype=I32))
    cs_s, sn_s = _rope_tables(jnp.full((SAMPLE_ROWS,), PAST_LEN, I32))
    ckv_p, kpe_p, ckv_s, kpe_s = [], [], [], []
    conv_p, conv_s, gla_p, gla_s = [], [], [], []
    for l in range(DEPTH):
        j = l // 2
        if l % 2 == 0:
            w_in, wg, w_out = _even_layer_weights(ab_w_in[j], gla_w_gate[j], ab_w_out[j])
            widths = (2 * D_A, 2 * D_BK + D_BV, D_BV, LANES)
            za, zqkv, zr, zg = norm_matmul(xp, ab_norm[j], w_in, widths)
            a_out, cbuf = conv_branch_prompt(za, conv_w[j], conv_b[j], conv_ln_g[j], conv_ln_b[j], bp)
            b_out, s_fin = gla_prompt(zqkv, zr, zg, wg, gla_b_gate[j], gla_head_g[j], bp)
            xp = even_out(xp, a_out, b_out, w_out)
            conv_p.append(cbuf)
            gla_p.append(s_fin)
            za, zqkv, zr, zg = norm_matmul(xs, ab_norm[j], w_in, widths)
            st_t = jnp.pad(jnp.transpose(state_conv[j], (1, 0, 2)), ((0, 0), (0, SAMPLE_ROWS - bs), (0, 0)))
            u, a_out, qg, kg, ag = even_sample_a(za, zqkv, zg, st_t, conv_w[j], conv_b[j], conv_ln_g[j],
                                                 conv_ln_b[j], wg, gla_b_gate[j])
            col = lambda t: t[:bs].reshape(bs, D_BK, 1)
            s_new, b_out = even_sample_b(col(ag), col(kg), col(qg),
                                         zqkv[:bs, 2 * D_BK:].reshape(bs, 1, D_BV),
                                         zr[:bs].reshape(bs, 1, D_BV), state_gla[j], gla_head_g[j])
            b_out = jnp.pad(b_out.reshape(bs, D_BV), ((0, SAMPLE_ROWS - bs), (0, 0)))
            xs = even_out(xs, a_out, b_out, w_out)
            conv_s.append(jnp.concatenate([state_conv[j][:, 1:], u[:bs, None, :]], axis=1))
            gla_s.append(s_new)
        else:
            w_in, wn, wa, wb, wuk, wuv, w_out = _odd_layer_weights(c_w_in[j], c_w_q_b[j], c_w_uk[j], c_w_uv[j],
                                                                   c_w_out[j])
            widths = (R_Q, R_KV, LANES, LANES)
            cq, ckv, kpe, kper = norm_matmul(xp, c_norm[j], w_in, widths)
            qc, kc, ckv_o, kpe_o = mla_prep(cq, ckv, kpe, kper, cs_p, sn_p, c_q_norm[j], c_kv_norm[j],
                                            wn, wa, wb, wuk, tp // ATT_TQ)
            ol = mla_attn_prompt(qc, kc, bp)
            xp = mla_out(xp, ol, wuv, w_out)
            ckv_p.append(ckv_o.reshape(bp, tp, R_KV))
            kpe_p.append(kpe_o[:, :D_ROPE].reshape(bp, tp, D_ROPE))
            cq, ckv, kpe, kper = norm_matmul(xs, c_norm[j], w_in, widths)
            qc, kc, ckv_o, kpe_o = mla_prep(cq, ckv, kpe, kper, cs_s, sn_s, c_q_norm[j], c_kv_norm[j],
                                            wn, wa, wb, wuk, 1)
            q_s = jnp.transpose(qc[0, :, :bs, :], (1, 0, 2))
            o_lat = mla_decode(page_table, q_s, kc[:bs].reshape(bs, 1, KC_W), cache_ckv, cache_kpe, j)
            ol = jnp.pad(jnp.transpose(o_lat, (1, 0, 2)), ((0, 0), (0, SAMPLE_ROWS - bs), (0, 0)))
            xs = mla_out(xs, ol.astype(BF16)[None], wuv, w_out)
            ckv_s.append(ckv_o[:bs].reshape(bs, 1, R_KV))
            kpe_s.append(kpe_o[:bs, :D_ROPE].reshape(bs, 1, D_ROPE))
        wq = peer_w_q[l].astype(BF16)
        keys = peer_keys[l].astype(BF16)
        u_tab = peer_u[l].astype(BF16)
        vt_tab = peer_v[l].astype(BF16)
        xp = peer(xp, ffn_norm[l], wq, keys, u_tab, vt_tab)
        xs = peer(xs, ffn_norm[l], wq, keys, u_tab, vt_tab)
    y_prompt = final_rms(xp, final_norm).reshape(bp, tp, D_MODEL)
    y_sample = final_rms(xs, final_norm)[:bs].reshape(bs, 1, D_MODEL)
    return (y_prompt, y_sample,
            jnp.stack(ckv_p), jnp.stack(kpe_p), jnp.stack(ckv_s), jnp.stack(kpe_s),
            jnp.stack(conv_p), jnp.stack(conv_s), jnp.stack(gla_p), jnp.stack(gla_s))
```

```python
import functools

import jax
import jax.numpy as jnp
from jax import lax
from jax.experimental import pallas as pl
from jax.experimental.pallas import tpu as pltpu

F32 = jnp.float32
BF16 = jnp.bfloat16
I32 = jnp.int32

D_MODEL = 1024
SEQ = 2048
DEPTH = 4
DEC_BATCH = 32
PAST_LEN = 16384
PAGE_SIZE = 128
EPS = 1e-6
D_A = 512
CONV_W = 31
H_B = 4
D_BV = 512
D_BK = 256
DK_B = 64
DV_B = 128
GATE_RANK = 16
GATE_TAU = 16.0
GLA_CHUNK = 64
GLA_SUB = 16
H_C = 16
D_NOPE = 64
D_ROPE = 32
D_V = 64
R_Q = 384
R_KV = 256
ROPE_BASE = 10000.0
MLA_SCALE = (D_NOPE + D_ROPE) ** -0.5
Q_SCALE = MLA_SCALE * 1.4426950408889634
H_P = 8
N_KEYS = 128
N_EXPERTS = N_KEYS * N_KEYS
D_HALF = 128
TOPK = 16

LANES = 128
SUBLANES = 8
SAMPLE_ROWS = 128
KC_W = R_KV + LANES
ATT_TQ = 128
ATT_TK = 512
DEC_PAGES = 16
VMEM_LIMIT = 56 * 1024 * 1024
NEG_INF = float("-inf")


def _cparams(sem):
    return pltpu.CompilerParams(dimension_semantics=sem, vmem_limit_bytes=VMEM_LIMIT)


def _dot(a, b):
    return jnp.dot(a, b, preferred_element_type=F32)


def _dot_nt(a, b):
    return lax.dot_general(a, b, (((1,), (1,)), ((), ())), preferred_element_type=F32)


def _dot_tn(a, b):
    return lax.dot_general(a, b, (((0,), (0,)), ((), ())), preferred_element_type=F32)


def _rms(x, g):
    return x * lax.rsqrt(jnp.mean(x * x, axis=-1, keepdims=True) + EPS) * g


def _split3(x):
    hi = x.astype(BF16)
    r1 = x - hi.astype(F32)
    mid = r1.astype(BF16)
    lo = (r1 - mid.astype(F32)).astype(BF16)
    return hi, mid, lo


def _const_spec(shape):
    nd = len(shape)
    return pl.BlockSpec(shape, lambda *_: (0,) * nd)


def _norm_mm_kernel(x_ref, g_ref, w_ref, *out_refs, widths):
    xn = _rms(x_ref[...], g_ref[...])
    y = _dot(xn.astype(BF16), w_ref[...])
    off = 0
    for o_ref, w in zip(out_refs, widths):
        o_ref[...] = y[:, off:off + w]
        off += w


def norm_matmul(x, g, w, widths):
    n, k = x.shape
    tm = min(512, n)
    return pl.pallas_call(
        functools.partial(_norm_mm_kernel, widths=widths),
        grid=(n // tm,),
        in_specs=[pl.BlockSpec((tm, k), lambda i: (i, 0)), _const_spec((1, k)), _const_spec(w.shape)],
        out_specs=[pl.BlockSpec((tm, wd), lambda i: (i, 0)) for wd in widths],
        out_shape=[jax.ShapeDtypeStruct((n, wd), F32) for wd in widths],
        compiler_params=_cparams(("parallel",)),
        name="norm_matmul",
    )(x, g.reshape(1, k), w)


def _final_norm_kernel(x_ref, g_ref, o_ref):
    o_ref[...] = _rms(x_ref[...], g_ref[...])


def final_rms(x, g):
    n, k = x.shape
    tm = min(512, n)
    return pl.pallas_call(
        _final_norm_kernel,
        grid=(n // tm,),
        in_specs=[pl.BlockSpec((tm, k), lambda i: (i, 0)), _const_spec((1, k))],
        out_specs=pl.BlockSpec((tm, k), lambda i: (i, 0)),
        out_shape=jax.ShapeDtypeStruct((n, k), F32),
        compiler_params=_cparams(("parallel",)),
        name="final_norm",
    )(x, g.reshape(1, k))


def _ln_swish(c, g, b):
    mu = jnp.mean(c, axis=-1, keepdims=True)
    xc = c - mu
    y = xc * lax.rsqrt(jnp.mean(xc * xc, axis=-1, keepdims=True) + EPS) * g + b
    return y * jax.nn.sigmoid(y)


CONV_HIST = 32


def _conv_kernel(za_ref, cw_ref, cb_ref, lg_ref, lb_ref, out_ref, buf_ref, ext_ref, *, tt, nt):
    t = pl.program_id(1)

    @pl.when(t == 0)
    def _():
        ext_ref[0:CONV_HIST, :] = jnp.zeros((CONV_HIST, D_A), F32)

    za = za_ref[...]
    ext_ref[CONV_HIST:CONV_HIST + tt, :] = za[:, :D_A] * jax.nn.sigmoid(za[:, D_A:])
    first = CONV_HIST - (CONV_W - 1)
    acc = jnp.zeros((tt, D_A), F32) + cb_ref[...]
    for w in range(CONV_W):
        acc = acc + ext_ref[pl.ds(first + w, tt), :] * cw_ref[w:w + 1, :]
    out_ref[...] = _ln_swish(acc, lg_ref[...], lb_ref[...])

    @pl.when(t == nt - 1)
    def _():
        buf_ref[...] = ext_ref[pl.ds(tt + first, CONV_W - 1), :]

    ext_ref[0:CONV_HIST, :] = ext_ref[tt:tt + CONV_HIST, :]


def conv_branch_prompt(za, cw, cb, lg, lb, batch):
    n = za.shape[0]
    seq = n // batch
    tt = 512
    nt = seq // tt
    return pl.pallas_call(
        functools.partial(_conv_kernel, tt=tt, nt=nt),
        grid=(batch, nt),
        in_specs=[pl.BlockSpec((tt, 2 * D_A), lambda b, t: (b * nt + t, 0)),
                  _const_spec((CONV_W, D_A)), _const_spec((1, D_A)), _const_spec((1, D_A)),
                  _const_spec((1, D_A))],
        out_specs=[pl.BlockSpec((tt, D_A), lambda b, t: (b * nt + t, 0)),
                   pl.BlockSpec((None, CONV_W - 1, D_A), lambda b, t: (b, 0, 0))],
        out_shape=[jax.ShapeDtypeStruct((n, D_A), F32),
                   jax.ShapeDtypeStruct((batch, CONV_W - 1, D_A), F32)],
        scratch_shapes=[pltpu.VMEM((tt + CONV_HIST, D_A), F32)],
        compiler_params=_cparams(("parallel", "arbitrary")),
        name="conv_branch_prompt",
    )(za, cw, cb.reshape(1, D_A), lg.reshape(1, D_A), lb.reshape(1, D_A))


def _log_decay(zg, wg, bg):
    x = _dot(zg.astype(BF16), wg) + bg
    return (jnp.minimum(x, 0.0) - jnp.log1p(jnp.exp(-jnp.abs(x)))) * (1.0 / GATE_TAU)


def _head_norm_gate(o, hg, r):
    outs = []
    for h in range(H_B):
        oh = o[:, h * DV_B:(h + 1) * DV_B]
        outs.append(oh * lax.rsqrt(jnp.mean(oh * oh, axis=-1, keepdims=True) + EPS) * hg)
    return jnp.concatenate(outs, axis=1) * (r * jax.nn.sigmoid(r))


def _gla_kernel(zqkv_ref, zr_ref, zg_ref, wg_ref, bg_ref, hg_ref, ltri_ref, ones_ref,
                bout_ref, sfin_ref, s_ref, *, tt, nt):
    t = pl.program_id(1)

    @pl.when(t == 0)
    def _():
        s_ref[...] = jnp.zeros((D_BK, D_BV), F32)

    zq = zqkv_ref[...]
    q = zq[:, 0:D_BK] * (DK_B ** -0.5)
    k = zq[:, D_BK:2 * D_BK]
    v = zq[:, 2 * D_BK:]
    la = _log_decay(zg_ref[...], wg_ref[...], bg_ref[...])
    la3 = _split3(la)
    ltri = ltri_ref[...]
    b = _dot(ltri, la3[0]) + _dot(ltri, la3[1]) + _dot(ltri, la3[2])
    ones = ones_ref[...]

    lane_head = lax.broadcasted_iota(I32, (GLA_SUB, D_BK), 1) // DK_B
    bd_mask = (lax.broadcasted_iota(I32, (D_BK, D_BV), 0) // DK_B
               == lax.broadcasted_iota(I32, (D_BK, D_BV), 1) // DV_B)
    o_chunks = []
    for c in range(tt // GLA_CHUNK):
        r0 = c * GLA_CHUNK
        bc = b[r0:r0 + GLA_CHUNK]
        qc = q[r0:r0 + GLA_CHUNK]
        kc = k[r0:r0 + GLA_CHUNK]
        vc = v[r0:r0 + GLA_CHUNK]
        vcb = vc.astype(BF16)
        s_prev = s_ref[...]
        o = _dot((qc * jnp.exp(bc)).astype(BF16), s_prev.astype(BF16))
        o_rows = []
        for i in range(GLA_CHUNK // GLA_SUB):
            i0 = i * GLA_SUB
            nk = i0 + GLA_SUB
            ref_b = bc[i0:i0 + 1, :]
            qi = qc[i0:nk] * jnp.exp(bc[i0:nk] - ref_b)
            ki = kc[0:nk] * jnp.exp(ref_b - bc[0:nk])
            qs = jnp.concatenate([jnp.where(lane_head == h, qi, 0.0) for h in range(H_B)], axis=0)
            att = _dot_nt(qs.astype(BF16), ki.astype(BF16))
            row_i = lax.broadcasted_iota(I32, (H_B * GLA_SUB, nk), 0) % GLA_SUB
            col_j = lax.broadcasted_iota(I32, (H_B * GLA_SUB, nk), 1)
            att = jnp.where(col_j <= row_i + i0, att, 0.0)
            res = _dot(att.astype(BF16), vcb[0:nk])
            o_rows.append(jnp.concatenate(
                [res[h * GLA_SUB:(h + 1) * GLA_SUB, h * DV_B:(h + 1) * DV_B] for h in range(H_B)], axis=1))
        o_chunks.append(o + jnp.concatenate(o_rows, axis=0))
        b_end = bc[GLA_CHUNK - 1:GLA_CHUNK, :]
        kst = kc * jnp.exp(b_end - bc)
        kv = _dot_tn(kst.astype(BF16), vcb)
        b_end_t = (_dot_tn(la3[0][r0:r0 + GLA_CHUNK], ones) + _dot_tn(la3[1][r0:r0 + GLA_CHUNK], ones)
                   + _dot_tn(la3[2][r0:r0 + GLA_CHUNK], ones))
        dec = jnp.exp(b_end_t)
        s_ref[...] = s_prev * jnp.concatenate([dec] * H_B, axis=1) + jnp.where(bd_mask, kv, 0.0)

    o_all = jnp.concatenate(o_chunks, axis=0)
    bout_ref[...] = _head_norm_gate(o_all, hg_ref[...], zr_ref[...])

    @pl.when(t == nt - 1)
    def _():
        for h in range(H_B):
            sfin_ref[h] = s_ref[h * DK_B:(h + 1) * DK_B, h * DV_B:(h + 1) * DV_B]


def gla_prompt(zqkv, zr, zg, wg, bg, hg, batch):
    n = zqkv.shape[0]
    seq = n // batch
    tt = 256
    nt = seq // tt
    ri = lax.broadcasted_iota(I32, (tt, tt), 0)
    ci = lax.broadcasted_iota(I32, (tt, tt), 1)
    ltri = ((ri // GLA_CHUNK == ci // GLA_CHUNK) & (ci <= ri)).astype(BF16)
    ones = jnp.ones((GLA_CHUNK, LANES), BF16)
    row = lambda b, t: (b * nt + t, 0)
    return pl.pallas_call(
        functools.partial(_gla_kernel, tt=tt, nt=nt),
        grid=(batch, nt),
        in_specs=[pl.BlockSpec((tt, 2 * D_BK + D_BV), row), pl.BlockSpec((tt, D_BV), row),
                  pl.BlockSpec((tt, LANES), row), _const_spec((LANES, D_BK)), _const_spec((1, D_BK)),
                  _const_spec((1, DV_B)), _const_spec((tt, tt)), _const_spec((GLA_CHUNK, LANES))],
        out_specs=[pl.BlockSpec((tt, D_BV), row),
                   pl.BlockSpec((None, H_B, DK_B, DV_B), lambda b, t: (b, 0, 0, 0))],
        out_shape=[jax.ShapeDtypeStruct((n, D_BV), F32),
                   jax.ShapeDtypeStruct((batch, H_B, DK_B, DV_B), F32)],
        scratch_shapes=[pltpu.VMEM((D_BK, D_BV), F32)],
        compiler_params=_cparams(("parallel", "arbitrary")),
        name="gla_prompt",
    )(zqkv, zr, zg, wg, bg.reshape(1, D_BK), hg.reshape(1, DV_B), ltri, ones)


def _even_sample_a_kernel(za_ref, zqkv_ref, zg_ref, st_ref, cw_ref, cb_ref, lg_ref, lb_ref, wg_ref, bg_ref,
                          u_ref, aout_ref, q_ref, k_ref, a_ref):
    za = za_ref[...]
    u = za[:, :D_A] * jax.nn.sigmoid(za[:, D_A:])
    u_ref[...] = u
    acc = jnp.zeros_like(u) + cb_ref[...]
    for w in range(CONV_W - 1):
        acc = acc + st_ref[w] * cw_ref[w:w + 1, :]
    acc = acc + u * cw_ref[CONV_W - 1:CONV_W, :]
    aout_ref[...] = _ln_swish(acc, lg_ref[...], lb_ref[...])
    zq = zqkv_ref[...]
    q_ref[...] = zq[:, 0:D_BK] * (DK_B ** -0.5)
    k_ref[...] = zq[:, D_BK:2 * D_BK]
    a_ref[...] = jnp.exp(_log_decay(zg_ref[...], wg_ref[...], bg_ref[...]))


def even_sample_a(za, zqkv, zg, st_t, cw, cb, lg, lb, wg, bg):
    n = za.shape[0]
    shp = lambda w: jax.ShapeDtypeStruct((n, w), F32)
    return pl.pallas_call(
        _even_sample_a_kernel,
        out_shape=[shp(D_A), shp(D_A), shp(D_BK), shp(D_BK), shp(D_BK)],
        compiler_params=pltpu.CompilerParams(vmem_limit_bytes=VMEM_LIMIT),
        name="even_sample_a",
    )(za, zqkv, zg, st_t, cw, cb.reshape(1, D_A), lg.reshape(1, D_A), lb.reshape(1, D_A), wg,
      bg.reshape(1, D_BK))


def _even_sample_b_kernel(a_ref, k_ref, q_ref, v_ref, r_ref, s_ref, hg_ref, snew_ref, bout_ref):
    v = v_ref[...]
    outs = []
    for h in range(H_B):
        rows = slice(h * DK_B, (h + 1) * DK_B)
        vh = v[:, h * DV_B:(h + 1) * DV_B]
        s_new = a_ref[rows, :] * s_ref[h] + k_ref[rows, :] * vh
        snew_ref[h] = s_new
        outs.append(jnp.sum(q_ref[rows, :] * s_new, axis=0, keepdims=True))
    bout_ref[...] = _head_norm_gate(jnp.concatenate(outs, axis=1), hg_ref[...], r_ref[...])


def even_sample_b(a3, k3, q3, v3, r3, state, hg):
    nb = a3.shape[0]
    col = pl.BlockSpec((None, D_BK, 1), lambda b: (b, 0, 0))
    rowv = pl.BlockSpec((None, 1, D_BV), lambda b: (b, 0, 0))
    st = pl.BlockSpec((None, H_B, DK_B, DV_B), lambda b: (b, 0, 0, 0))
    return pl.pallas_call(
        _even_sample_b_kernel,
        grid=(nb,),
        in_specs=[col, col, col, rowv, rowv, st, _const_spec((1, DV_B))],
        out_specs=[st, rowv],
        out_shape=[jax.ShapeDtypeStruct((nb, H_B, DK_B, DV_B), F32),
                   jax.ShapeDtypeStruct((nb, 1, D_BV), F32)],
        compiler_params=_cparams(("parallel",)),
        name="even_sample_b",
    )(a3, k3, q3, v3, r3, state, hg.reshape(1, DV_B))


def _even_out_kernel(x_ref, a_ref, b_ref, w_ref, o_ref):
    y = _dot(a_ref[...].astype(BF16), w_ref[0:D_A, :]) + _dot(b_ref[...].astype(BF16), w_ref[D_A:, :])
    o_ref[...] = x_ref[...] + y


def even_out(x, a, b, w):
    n = x.shape[0]
    tm = min(512, n)
    row = lambda i: (i, 0)
    return pl.pallas_call(
        _even_out_kernel,
        grid=(n // tm,),
        in_specs=[pl.BlockSpec((tm, D_MODEL), row), pl.BlockSpec((tm, D_A), row),
                  pl.BlockSpec((tm, D_BV), row), _const_spec(w.shape)],
        out_specs=pl.BlockSpec((tm, D_MODEL), row),
        out_shape=jax.ShapeDtypeStruct((n, D_MODEL), F32),
        compiler_params=_cparams(("parallel",)),
        name="even_out",
    )(x, a, b, w)


def _mla_prep_kernel(cq_ref, ckv_ref, kpe_ref, kper_ref, cs_ref, sn_ref, qg_ref, kvg_ref,
                     wn_ref, wa_ref, wb_ref, wuk_ref, qc_ref, kc_ref, ckvo_ref, kpeo_ref):
    cs = cs_ref[...]
    sn = sn_ref[...]
    ckv = _rms(ckv_ref[...], kvg_ref[...])
    kpe = kpe_ref[...] * cs + kper_ref[...] * sn
    ckvo_ref[...] = ckv
    kpeo_ref[...] = kpe
    kc_ref[:, 0:R_KV] = ckv.astype(BF16)
    kc_ref[:, R_KV:] = kpe.astype(BF16)
    cqn = _rms(cq_ref[...], qg_ref[...]).astype(BF16)
    qn = _dot(cqn, wn_ref[...]).astype(BF16)
    qa = _dot(cqn, wa_ref[...])
    qb = _dot(cqn, wb_ref[...])
    for h in range(H_C):
        pair = qn[:, (h // 2) * LANES:(h // 2 + 1) * LANES]
        ql = (_dot(pair, wuk_ref[h]) * Q_SCALE).astype(BF16)
        qp = ((qa[:, h * LANES:(h + 1) * LANES] * cs + qb[:, h * LANES:(h + 1) * LANES] * sn)
              * Q_SCALE).astype(BF16)
        for t in range(qc_ref.shape[0]):
            tok = slice(t * ATT_TQ, (t + 1) * ATT_TQ)
            qc_ref[t, h, :, 0:R_KV] = ql[tok]
            qc_ref[t, h, :, R_KV:] = qp[tok]


def mla_prep(cq, ckv, kpe, kper, cs, sn, qg, kvg, wn, wa, wb, wuk, pos_tiles):
    n = cq.shape[0]
    tm = min(512, n)
    sub = tm // ATT_TQ
    row = lambda i: (i, 0)
    pos = lambda i: (i % (pos_tiles // sub), 0)
    return pl.pallas_call(
        _mla_prep_kernel,
        grid=(n // tm,),
        in_specs=[pl.BlockSpec((tm, R_Q), row), pl.BlockSpec((tm, R_KV), row),
                  pl.BlockSpec((tm, LANES), row), pl.BlockSpec((tm, LANES), row),
                  pl.BlockSpec((tm, LANES), pos), pl.BlockSpec((tm, LANES), pos),
                  _const_spec((1, R_Q)), _const_spec((1, R_KV)),
                  _const_spec(wn.shape), _const_spec(wa.shape), _const_spec(wb.shape), _const_spec(wuk.shape)],
        out_specs=[pl.BlockSpec((sub, H_C, ATT_TQ, KC_W), lambda i: (i, 0, 0, 0)),
                   pl.BlockSpec((tm, KC_W), row), pl.BlockSpec((tm, R_KV), row),
                   pl.BlockSpec((tm, LANES), row)],
        out_shape=[jax.ShapeDtypeStruct((n // ATT_TQ, H_C, ATT_TQ, KC_W), BF16),
                   jax.ShapeDtypeStruct((n, KC_W), BF16),
                   jax.ShapeDtypeStruct((n, R_KV), F32),
                   jax.ShapeDtypeStruct((n, LANES), F32)],
        compiler_params=_cparams(("parallel",)),
        name="mla_prep",
    )(cq, ckv, kpe, kper, cs, sn, qg.reshape(1, R_Q), kvg.reshape(1, R_KV), wn, wa, wb, wuk)


ATT_STRIP = H_C * ATT_TQ


def _attn_strip_update(q, kc, vt, visible, carry, acc_ref):
    m_old, l_old = carry
    st = _dot_nt(kc, q)
    if visible is not None:
        st = jnp.where(visible, st, NEG_INF)
    m_new = jnp.maximum(m_old, jnp.max(st, axis=0, keepdims=True))
    p = jnp.exp2(st - m_new)
    alpha = jnp.exp2(m_old - m_new)
    acc_ref[...] = alpha * acc_ref[...] + _dot(vt, p.astype(BF16))
    return m_new, alpha * l_old + jnp.sum(p, axis=0, keepdims=True)


def _mla_attn_kernel(qc_ref, kc_ref, vt_ref, ol_ref, acc_ref):
    i = pl.program_id(1)
    n_full = (i * ATT_TQ) // ATT_TK
    k0 = pl.multiple_of(n_full * ATT_TK, ATT_TK)
    tok = i * ATT_TQ + lax.broadcasted_iota(I32, (ATT_TK, ATT_STRIP), 1) % ATT_TQ
    key = k0 + lax.broadcasted_iota(I32, (ATT_TK, ATT_STRIP), 0)
    causal = key <= tok
    heads = ATT_STRIP // ATT_TQ
    n_strip = H_C // heads
    qs = [qc_ref[heads * s:heads * (s + 1)].reshape(ATT_STRIP, KC_W) for s in range(n_strip)]
    acc_ref[...] = jnp.zeros((n_strip, R_KV, ATT_STRIP), F32)

    def full_block(j, carry):
        kc = kc_ref[pl.ds(pl.multiple_of(j * ATT_TK, ATT_TK), ATT_TK), :]
        vt = vt_ref[j]
        return tuple(_attn_strip_update(qs[s], kc, vt, None, carry[s], acc_ref.at[s]) for s in range(n_strip))

    init = (jnp.full((1, ATT_STRIP), NEG_INF, F32), jnp.zeros((1, ATT_STRIP), F32))
    carry = lax.fori_loop(0, n_full, full_block, (init,) * n_strip)
    kc = kc_ref[pl.ds(k0, ATT_TK), :]
    vt = vt_ref[n_full]
    for s in range(n_strip):
        _, l_fin = _attn_strip_update(qs[s], kc, vt, causal, carry[s], acc_ref.at[s])
        o = (acc_ref[s] / l_fin).T
        ol_ref[heads * s:heads * (s + 1)] = o.reshape(heads, ATT_TQ, R_KV).astype(BF16)


def mla_attn_prompt(qc, kc, batch):
    ntile = qc.shape[0]
    nq = ntile // batch
    seq = nq * ATT_TQ
    nk = seq // ATT_TK
    vt = jnp.transpose(kc[:, :R_KV].reshape(batch, nk, ATT_TK, R_KV), (0, 1, 3, 2))
    return pl.pallas_call(
        _mla_attn_kernel,
        grid=(batch, nq),
        in_specs=[pl.BlockSpec((None, H_C, ATT_TQ, KC_W), lambda b, i: (b * nq + i, 0, 0, 0)),
                  pl.BlockSpec((seq, KC_W), lambda b, i: (b, 0)),
                  pl.BlockSpec((None, nk, R_KV, ATT_TK), lambda b, i: (b, 0, 0, 0))],
        out_specs=pl.BlockSpec((None, H_C, ATT_TQ, R_KV), lambda b, i: (b * nq + i, 0, 0, 0)),
        out_shape=jax.ShapeDtypeStruct((ntile, H_C, ATT_TQ, R_KV), BF16),
        scratch_shapes=[pltpu.VMEM((H_C * ATT_TQ // ATT_STRIP, R_KV, ATT_STRIP), F32)],
        compiler_params=_cparams(("parallel", "arbitrary")),
        name="mla_attn_prompt",
    )(qc, kc, vt)


def _mla_decode_kernel(pt_ref, q_ref, kcur_ref, *refs, ng):
    ckv_refs = refs[:DEC_PAGES]
    kpe_refs = refs[DEC_PAGES:2 * DEC_PAGES]
    o_ref, m_ref, l_ref, acc_ref = refs[2 * DEC_PAGES:]
    g = pl.program_id(1)

    @pl.when(g == 0)
    def _():
        m_ref[...] = jnp.full((H_C, 1), NEG_INF, F32)
        l_ref[...] = jnp.zeros((H_C, 1), F32)
        acc_ref[...] = jnp.zeros((H_C, R_KV), F32)

    q = q_ref[...]
    ql = q[:, 0:R_KV]
    qp = q[:, R_KV:R_KV + D_ROPE]
    cks = [r[...].astype(BF16) for r in ckv_refs]
    s = jnp.concatenate(
        [_dot_nt(ql, cks[p]) + _dot_nt(qp, kpe_refs[p][...].astype(BF16)) for p in range(DEC_PAGES)], axis=1)
    m_old = m_ref[...]
    m_new = jnp.maximum(m_old, jnp.max(s, axis=1, keepdims=True))
    p_all = jnp.exp2(s - m_new).astype(BF16)
    alpha = jnp.exp2(m_old - m_new)
    pv = _dot(p_all[:, 0:PAGE_SIZE], cks[0])
    for p in range(1, DEC_PAGES):
        pv = pv + _dot(p_all[:, p * PAGE_SIZE:(p + 1) * PAGE_SIZE], cks[p])
    l_new = alpha * l_ref[...] + jnp.sum(p_all.astype(F32), axis=1, keepdims=True)
    acc_new = alpha * acc_ref[...] + pv
    m_ref[...] = m_new
    l_ref[...] = l_new
    acc_ref[...] = acc_new

    @pl.when(g == ng - 1)
    def _():
        kcur = kcur_ref[...].astype(F32)
        s_cur = jnp.sum(q.astype(F32) * kcur, axis=1, keepdims=True)
        m_fin = jnp.maximum(m_new, s_cur)
        a_fin = jnp.exp2(m_new - m_fin)
        p_cur = jnp.exp2(s_cur - m_fin)
        l_fin = a_fin * l_new + p_cur
        o_ref[...] = (a_fin * acc_new + p_cur * kcur[:, 0:R_KV]) / l_fin


def mla_decode(page_table, q_s, kc_s, cache_ckv, cache_kpe, layer):
    nb, npages = page_table.shape
    ng = npages // DEC_PAGES

    def page_spec(width, p):
        return pl.BlockSpec((None, None, PAGE_SIZE, width),
                            lambda b, g, pt: (layer, pt[b, g * DEC_PAGES + p], 0, 0))

    grid_spec = pltpu.PrefetchScalarGridSpec(
        num_scalar_prefetch=1,
        grid=(nb, ng),
        in_specs=([pl.BlockSpec((None, H_C, KC_W), lambda b, g, pt: (b, 0, 0)),
                   pl.BlockSpec((None, 1, KC_W), lambda b, g, pt: (b, 0, 0))]
                  + [page_spec(R_KV, p) for p in range(DEC_PAGES)]
                  + [page_spec(D_ROPE, p) for p in range(DEC_PAGES)]),
        out_specs=pl.BlockSpec((None, H_C, R_KV), lambda b, g, pt: (b, 0, 0)),
        scratch_shapes=[pltpu.VMEM((H_C, 1), F32), pltpu.VMEM((H_C, 1), F32), pltpu.VMEM((H_C, R_KV), F32)],
    )
    return pl.pallas_call(
        functools.partial(_mla_decode_kernel, ng=ng),
        grid_spec=grid_spec,
        out_shape=jax.ShapeDtypeStruct((nb, H_C, R_KV), F32),
        compiler_params=_cparams(("parallel", "arbitrary")),
        name="mla_decode",
    )(page_table, q_s, kc_s, *([cache_ckv] * DEC_PAGES), *([cache_kpe] * DEC_PAGES))


def _mla_out_kernel(x_ref, ol_ref, wuv_ref, wo_ref, o_ref):
    rows = x_ref.shape[0]
    head = lambda h: ol_ref[:, h].reshape(rows, R_KV)
    pairs = []
    for p in range(H_C // 2):
        pairs.append((_dot(head(2 * p), wuv_ref[2 * p]) + _dot(head(2 * p + 1), wuv_ref[2 * p + 1])).astype(BF16))
    o = jnp.concatenate(pairs, axis=1)
    o_ref[...] = x_ref[...] + _dot(o, wo_ref[...])


def mla_out(x, ol, wuv, wo):
    n = x.shape[0]
    tm = min(512, n)
    return pl.pallas_call(
        _mla_out_kernel,
        grid=(n // tm,),
        in_specs=[pl.BlockSpec((tm, D_MODEL), lambda i: (i, 0)),
                  pl.BlockSpec((tm // ATT_TQ, H_C, ATT_TQ, R_KV), lambda i: (i, 0, 0, 0)),
                  _const_spec(wuv.shape), _const_spec(wo.shape)],
        out_specs=pl.BlockSpec((tm, D_MODEL), lambda i: (i, 0)),
        out_shape=jax.ShapeDtypeStruct((n, D_MODEL), F32),
        compiler_params=_cparams(("parallel",)),
        name="mla_out",
    )(x, ol, wuv, wo)


def _top16_rows(s):
    rows, tm = s.shape
    iota = lax.broadcasted_iota(I32, (rows, tm), 0)
    rank = jnp.full((rows, tm), TOPK, I32)
    vals = []
    cur = s
    for it in range(TOPK):
        m = jnp.max(cur, axis=0, keepdims=True)
        idx = jnp.min(jnp.where(cur == m, iota, rows), axis=0, keepdims=True)
        sel = iota == idx
        rank = jnp.where(sel, it, rank)
        cur = jnp.where(sel, NEG_INF, cur)
        vals.append(m)
    return jnp.concatenate(vals, axis=0), rank


CAND_ROWS = 80


def _pair_top16(v1, v2):
    cand, _, flat = _pair_candidates(v1, v2)
    sums, a_sel = [], []
    for _ in range(TOPK):
        m = jnp.max(cand, axis=0, keepdims=True)
        ci = jnp.min(jnp.where(cand == m, flat, TOPK * TOPK), axis=0, keepdims=True)
        cand = jnp.where(flat == ci, NEG_INF, cand)
        sums.append(m)
        a_sel.append(ci >> 4)
    return sums, a_sel


def _top16_rows_distinct(s):
    rows, tm = s.shape
    rank = jnp.full((rows, tm), float(TOPK), F32)
    vals = []
    cur = s
    for it in range(TOPK):
        m = jnp.max(cur, axis=0, keepdims=True)
        sel = cur == m
        rank = jnp.where(sel, float(it), rank)
        cur = jnp.where(sel, NEG_INF, cur)
        vals.append(m)
    marked = jnp.sum(jnp.where(rank < float(TOPK), 1.0, 0.0), axis=0, keepdims=True)
    return jnp.concatenate(vals, axis=0), rank, marked


def _pair_candidates(v1, v2):
    tm = v1.shape[1]
    pieces = [v1[0:1] + v2]
    for a in range(1, 8):
        pieces.append(v1[a:a + 1] + v2[0:8])
    pieces.append(v1[8:16] + v2[0:1])
    cand = jnp.concatenate(pieces, axis=0)
    r = lax.broadcasted_iota(I32, (CAND_ROWS, tm), 0)
    a_idx = jnp.where(r < 16, 0, jnp.where(r < 72, 1 + ((r - 16) >> 3), r - 64))
    b_idx = jnp.where(r < 16, r, jnp.where(r < 72, (r - 16) & 7, 0))
    cand = jnp.where((a_idx + 1) * (b_idx + 1) <= TOPK, cand, NEG_INF)
    return cand, a_idx, a_idx * TOPK + b_idx


def _pair_top16_distinct(v1, v2):
    cand, _, _ = _pair_candidates(v1, v2)
    cur = cand
    sums = []
    for _ in range(TOPK):
        m = jnp.max(cur, axis=0, keepdims=True)
        cur = jnp.where(cur == m, NEG_INF, cur)
        sums.append(m)
    picked = jnp.where((cur == NEG_INF) & (cand > NEG_INF), 1.0, 0.0)
    counts = [jnp.sum(picked[0:16], axis=0, keepdims=True)]
    for a in range(1, 8):
        counts.append(jnp.sum(picked[8 + 8 * a:16 + 8 * a], axis=0, keepdims=True))
    cnt = jnp.concatenate(counts + [picked[72:80]], axis=0)
    return sums, cnt, jnp.sum(picked, axis=0, keepdims=True)


def _bf16_twice(x):
    hi = pltpu.bitcast(x.astype(BF16).astype(F32), jnp.uint32)
    return hi | (hi >> 16)


def _route_tables(s1, s2, rank1, rank2, v1, v2, sums, cnt, refs, h):
    r2_ref, bt_ref, e1_ref, e2_ref = refs
    z = jnp.exp(sums[0] - sums[0])
    for kk in range(1, TOPK):
        z = z + jnp.exp(sums[kk] - sums[0])
    bt = jnp.zeros(s1.shape, F32)
    for a in range(TOPK):
        bt = jnp.where(rank1 == a, cnt[a:a + 1], bt)
    e1 = jnp.exp(s1 - v1[0:1]) * (1.0 / z)
    e2 = jnp.exp(s2 - v2[0:1])
    r2 = rank2.astype(F32)
    for c in range(s1.shape[1] // LANES):
        cols = slice(c * LANES, (c + 1) * LANES)
        bt_ref[h, c] = _bf16_twice(bt[:, cols])
        e1_ref[h, c] = _bf16_twice(e1[:, cols])
        e2_ref[h, c] = pltpu.bitcast(e2[:, cols].astype(BF16), jnp.uint32)
        r2_ref[h, c] = pltpu.bitcast(r2[:, cols].astype(BF16), jnp.uint32)


def _peer_route_kernel(x_ref, g_ref, wq_ref, keys_ref, xn_ref, r2_ref, bt_ref, e1_ref, e2_ref, q_ref, bad_ref):
    xn = _rms(x_ref[...], g_ref[...]).astype(BF16)
    xn_ref[...] = xn
    q = _dot(xn, wq_ref[...]).astype(BF16)
    tm = q.shape[0]
    for k in range(2 * H_P):
        q_ref[k] = q[:, k * D_HALF:(k + 1) * D_HALF]
    refs = (r2_ref, bt_ref, e1_ref, e2_ref)

    def scores(h):
        return _dot_nt(keys_ref[h, 0], q_ref[2 * h]), _dot_nt(keys_ref[h, 1], q_ref[2 * h + 1])

    for h in range(H_P):
        s1, s2 = scores(h)
        v1, rank1, n1 = _top16_rows_distinct(s1)
        v2, rank2, n2 = _top16_rows_distinct(s2)
        sums, cnt, n3 = _pair_top16_distinct(v1, v2)
        _route_tables(s1, s2, rank1, rank2, v1, v2, sums, cnt, refs, h)
        bad_ref[h:h + 1, :] = jnp.abs(n1 - TOPK) + jnp.abs(n2 - TOPK) + jnp.abs(n3 - TOPK)

    def redo(h, carry):
        @pl.when(jnp.max(bad_ref[pl.ds(h, 1), :]) > 0.0)
        def _():
            iota16 = lax.broadcasted_iota(I32, (TOPK, tm), 0)
            s1, s2 = scores(h)
            v1, rank1 = _top16_rows(s1)
            v2, rank2 = _top16_rows(s2)
            sums, a_sel = _pair_top16(v1, v2)
            cnt = jnp.zeros((TOPK, tm), F32)
            for kk in range(TOPK):
                cnt = cnt + jnp.where(iota16 == a_sel[kk], 1.0, 0.0)
            _route_tables(s1, s2, rank1, rank2, v1, v2, sums, cnt, refs, h)
        return carry

    lax.fori_loop(0, H_P, redo, 0)


def peer_route(x, g, wq, keys):
    n = x.shape[0]
    tm = min(256, n)
    tab = lambda rows: pl.BlockSpec((H_P, tm // LANES, rows, LANES), lambda i: (0, i, 0, 0))
    u32 = jnp.uint32
    tab_shape = lambda rows, dt: jax.ShapeDtypeStruct((H_P, n // LANES, rows, LANES), dt)
    return pl.pallas_call(
        _peer_route_kernel,
        grid=(n // tm,),
        in_specs=[pl.BlockSpec((tm, D_MODEL), lambda i: (i, 0)), _const_spec((1, D_MODEL)),
                  _const_spec(wq.shape), _const_spec(keys.shape)],
        out_specs=[pl.BlockSpec((tm, D_MODEL), lambda i: (i, 0)), tab(N_KEYS // 2), tab(N_KEYS), tab(N_KEYS),
                   tab(N_KEYS // 2)],
        out_shape=[jax.ShapeDtypeStruct((n, D_MODEL), BF16), tab_shape(N_KEYS // 2, u32),
                   tab_shape(N_KEYS, u32), tab_shape(N_KEYS, u32), tab_shape(N_KEYS // 2, u32)],
        scratch_shapes=[pltpu.VMEM((2 * H_P, tm, D_HALF), BF16), pltpu.VMEM((H_P, tm), F32)],
        compiler_params=_cparams(("parallel",)),
        name="peer_route",
    )(x, g.reshape(1, D_MODEL), wq, keys)


def _peer_dense_kernel(x_ref, xn_ref, r2_ref, bt_ref, e1_ref, e2_ref, u_ref, v_ref, o_ref,
                       ht_ref, act_ref, *, tm, te, ne):
    e = pl.program_id(1)

    @pl.when(e == 0)
    def _():
        o_ref[...] = x_ref[...]

    assert te == SUBLANES * N_KEYS
    xn = xn_ref[...]

    def project(pair):
        rows = slice(2 * pair * N_KEYS, (2 * pair + 2) * N_KEYS)
        ht_ref[rows, :] = _dot_nt(u_ref[rows, :], xn)

    groups = N_KEYS // (2 * SUBLANES)

    def token_row(ref, h, c, ii):
        word = jnp.broadcast_to(ref[h, c, e, ii:ii + 1, :], (SUBLANES, LANES))
        return pltpu.bitcast(word, BF16)[None]

    project(0)
    for pair in range(SUBLANES // 2):
        iis = (2 * pair, 2 * pair + 1)
        if pair + 1 < SUBLANES // 2:
            project(pair + 1)
        for c in range(tm // LANES):
            cols = slice(c * LANES, (c + 1) * LANES)
            gates = [jnp.zeros((groups, 2 * SUBLANES, LANES), BF16) for _ in iis]
            for h in range(H_P):
                r2 = pltpu.bitcast(r2_ref[h, c], BF16).reshape(groups, 2 * SUBLANES, LANES)
                e2 = pltpu.bitcast(e2_ref[h, c], BF16).reshape(groups, 2 * SUBLANES, LANES)
                for k, ii in enumerate(iis):
                    val = e2 * token_row(e1_ref, h, c, ii)
                    gates[k] = gates[k] + jnp.where(r2 < token_row(bt_ref, h, c, ii), val, jnp.zeros_like(val))
            for k, ii in enumerate(iis):
                rows = slice(ii * N_KEYS, (ii + 1) * N_KEYS)
                ht = ht_ref[rows, cols]
                gelu = (0.5 * ht * (1.0 + lax.erf(ht * (2.0 ** -0.5)))).astype(BF16)
                act_ref[rows, cols] = gelu * gates[k].reshape(N_KEYS, LANES)
    o_ref[...] += _dot_tn(act_ref[...], v_ref[...])


def peer_dense(x, xn, r2, bt, e1, e2, u, vt):
    n = x.shape[0]
    tm = min(1024, n)
    te = SUBLANES * N_KEYS
    ne = N_EXPERTS // te
    once = dict(pipeline_mode=pl.Buffered(1))
    f32_view = (H_P, n // LANES, N_KEYS // SUBLANES, SUBLANES, LANES)
    bf16_view = (H_P, n // LANES, N_KEYS // 2, LANES)
    tab = lambda view: pl.BlockSpec((H_P, tm // LANES) + view[2:], lambda i, e: (0, i) + (0,) * (len(view) - 2),
                                    **once)
    return pl.pallas_call(
        functools.partial(_peer_dense_kernel, tm=tm, te=te, ne=ne),
        grid=(n // tm, ne),
        in_specs=[pl.BlockSpec((tm, D_MODEL), lambda i, e: (i, 0), **once),
                  pl.BlockSpec((tm, D_MODEL), lambda i, e: (i, 0), **once),
                  tab(bf16_view), tab(f32_view), tab(f32_view), tab(bf16_view),
                  pl.BlockSpec((te, D_MODEL), lambda i, e: (e, 0)),
                  pl.BlockSpec((te, D_MODEL), lambda i, e: (e, 0))],
        out_specs=pl.BlockSpec((tm, D_MODEL), lambda i, e: (i, 0)),
        out_shape=jax.ShapeDtypeStruct((n, D_MODEL), F32),
        scratch_shapes=[pltpu.VMEM((te, tm), F32), pltpu.VMEM((te, tm), BF16)],
        compiler_params=_cparams(("parallel", "arbitrary")),
        name="peer_dense",
    )(x, xn, r2.reshape(bf16_view), bt.reshape(f32_view), e1.reshape(f32_view), e2.reshape(bf16_view), u, vt)


def peer(x, g, wq, keys, u, vt):
    xn, r2, bt, e1, e2 = peer_route(x, g, wq, keys)
    return peer_dense(x, xn, r2, bt, e1, e2, u, vt)


def _pad_cols(w, width):
    return jnp.pad(w, ((0, 0), (0, width - w.shape[1])))


def _rope_tables(pos):
    half = D_ROPE // 2
    inv = jnp.power(ROPE_BASE, -jnp.arange(half, dtype=F32) / half)
    ang = pos.astype(F32)[:, None] * inv[None, :]
    cos2 = jnp.concatenate([jnp.cos(ang), jnp.cos(ang)], axis=1)
    sin2 = jnp.concatenate([jnp.sin(ang), jnp.sin(ang)], axis=1)
    return _pad_cols(cos2, LANES), _pad_cols(sin2, LANES)


def _rot_half_cols(w):
    half = D_ROPE // 2
    return jnp.concatenate([-w[..., half:], w[..., :half]], axis=-1)


def _even_layer_weights(w_in, w_gate, w_out):
    w_in_p = _pad_cols(w_in, 2 * D_A + 2 * D_BK + 2 * D_BV + LANES).astype(BF16)
    wg = jnp.pad(w_gate, ((0, LANES - GATE_RANK), (0, 0))).astype(BF16)
    return w_in_p, wg, w_out.astype(BF16)


def _odd_layer_weights(w_in, w_q_b, w_uk, w_uv, w_out):
    wk = w_in[:, R_Q + R_KV:]
    w_in_p = jnp.concatenate([w_in[:, :R_Q + R_KV], _pad_cols(wk, LANES), _pad_cols(_rot_half_cols(wk), LANES)],
                             axis=1).astype(BF16)
    wq3 = w_q_b.reshape(R_Q, H_C, D_NOPE + D_ROPE)
    wn = wq3[:, :, :D_NOPE].reshape(R_Q, H_C * D_NOPE).astype(BF16)
    wpe = wq3[:, :, D_NOPE:]
    pad3 = lambda w: jnp.pad(w, ((0, 0), (0, 0), (0, LANES - D_ROPE))).reshape(R_Q, H_C * LANES).astype(BF16)
    wa, wb = pad3(wpe), pad3(_rot_half_cols(wpe))
    ukt = jnp.transpose(w_uk, (1, 2, 0))
    uv = jnp.transpose(w_uv, (1, 0, 2))
    odd = (jnp.arange(H_C) % 2 == 1)[:, None, None]
    zk = jnp.zeros_like(ukt)
    wuk = jnp.where(odd, jnp.concatenate([zk, ukt], axis=1), jnp.concatenate([ukt, zk], axis=1)).astype(BF16)
    zv = jnp.zeros_like(uv)
    wuv = jnp.where(odd, jnp.concatenate([zv, uv], axis=2), jnp.concatenate([uv, zv], axis=2)).astype(BF16)
    return w_in_p, wn, wa, wb, wuk, wuv, w_out.astype(BF16)


def kernel(x_prompt, x_sample, cache_ckv, cache_kpe, state_conv, state_gla, page_table, ab_norm, ab_w_in,
           conv_w, conv_b, conv_ln_g, conv_ln_b, gla_w_gate, gla_b_gate, gla_head_g, ab_w_out, c_norm, c_w_in,
           c_q_norm, c_kv_norm, c_w_q_b, c_w_uk, c_w_uv, c_w_out, ffn_norm, peer_w_q, peer_keys, peer_u,
           peer_v, final_norm):
    bp, tp, _ = x_prompt.shape
    bs = x_sample.shape[0]
    n_p = bp * tp
    xp = x_prompt.reshape(n_p, D_MODEL)
    xs = jnp.pad(x_sample.reshape(bs, D_MODEL), ((0, SAMPLE_ROWS - bs), (0, 0)))
    cs_p, sn_p = _rope_tables(jnp.arange(tp, dtype=I32))
    cs_s, sn_s = _rope_tables(jnp.full((SAMPLE_ROWS,), PAST_LEN, I32))
    ckv_p, kpe_p, ckv_s, kpe_s = [], [], [], []
    conv_p, conv_s, gla_p, gla_s = [], [], [], []
    for l in range(DEPTH):
        j = l // 2
        if l % 2 == 0:
            w_in, wg, w_out = _even_layer_weights(ab_w_in[j], gla_w_gate[j], ab_w_out[j])
            widths = (2 * D_A, 2 * D_BK + D_BV, D_BV, LANES)
            za, zqkv, zr, zg = norm_matmul(xp, ab_norm[j], w_in, widths)
            a_out, cbuf = conv_branch_prompt(za, conv_w[j], conv_b[j], conv_ln_g[j], conv_ln_b[j], bp)
            b_out, s_fin = gla_prompt(zqkv, zr, zg, wg, gla_b_gate[j], gla_head_g[j], bp)
            xp = even_out(xp, a_out, b_out, w_out)
            conv_p.append(cbuf)
            gla_p.append(s_fin)
            za, zqkv, zr, zg = norm_matmul(xs, ab_norm[j], w_in, widths)
            st_t = jnp.pad(jnp.transpose(state_conv[j], (1, 0, 2)), ((0, 0), (0, SAMPLE_ROWS - bs), (0, 0)))
            u, a_out, qg, kg, ag = even_sample_a(za, zqkv, zg, st_t, conv_w[j], conv_b[j], conv_ln_g[j],
                                                 conv_ln_b[j], wg, gla_b_gate[j])
            col = lambda t: t[:bs].reshape(bs, D_BK, 1)
            s_new, b_out = even_sample_b(col(ag), col(kg), col(qg),
                                         zqkv[:bs, 2 * D_BK:].reshape(bs, 1, D_BV),
                                         zr[:bs].reshape(bs, 1, D_BV), state_gla[j], gla_head_g[j])
            b_out = jnp.pad(b_out.reshape(bs, D_BV), ((0, SAMPLE_ROWS - bs), (0, 0)))
            xs = even_out(xs, a_out, b_out, w_out)
            conv_s.append(jnp.concatenate([state_conv[j][:, 1:], u[:bs, None, :]], axis=1))
            gla_s.append(s_new)
        else:
            w_in, wn, wa, wb, wuk, wuv, w_out = _odd_layer_weights(c_w_in[j], c_w_q_b[j], c_w_uk[j], c_w_uv[j],
                                                                   c_w_out[j])
            widths = (R_Q, R_KV, LANES, LANES)
            cq, ckv, kpe, kper = norm_matmul(xp, c_norm[j], w_in, widths)
            qc, kc, ckv_o, kpe_o = mla_prep(cq, ckv, kpe, kper, cs_p, sn_p, c_q_norm[j], c_kv_norm[j],
                                            wn, wa, wb, wuk, tp // ATT_TQ)
            ol = mla_attn_prompt(qc, kc, bp)
            xp = mla_out(xp, ol, wuv, w_out)
            ckv_p.append(ckv_o.reshape(bp, tp, R_KV))
            kpe_p.append(kpe_o[:, :D_ROPE].reshape(bp, tp, D_ROPE))
            cq, ckv, kpe, kper = norm_matmul(xs, c_norm[j], w_in, widths)
            qc, kc, ckv_o, kpe_o = mla_prep(cq, ckv, kpe, kper, cs_s, sn_s, c_q_norm[j], c_kv_norm[j],
                                            wn, wa, wb, wuk, 1)
            q_s = jnp.transpose(qc[0, :, :bs, :], (1, 0, 2))
            o_lat = mla_decode(page_table, q_s, kc[:bs].reshape(bs, 1, KC_W), cache_ckv, cache_kpe, j)
            ol = jnp.pad(jnp.transpose(o_lat, (1, 0, 2)), ((0, 0), (0, SAMPLE_ROWS - bs), (0, 0)))
            xs = mla_out(xs, ol.astype(BF16)[None], wuv, w_out)
            ckv_s.append(ckv_o[:bs].reshape(bs, 1, R_KV))
            kpe_s.append(kpe_o[:bs, :D_ROPE].reshape(bs, 1, D_ROPE))
        wq = peer_w_q[l].astype(BF16)
        keys = peer_keys[l].astype(BF16)
        u_tab = peer_u[l].astype(BF16)
        vt_tab = peer_v[l].astype(BF16)
        xp = peer(xp, ffn_norm[l], wq, keys, u_tab, vt_tab)
        xs = peer(xs, ffn_norm[l], wq, keys, u_tab, vt_tab)
    y_prompt = final_rms(xp, final_norm).reshape(bp, tp, D_MODEL)
    y_sample = final_rms(xs, final_norm)[:bs].reshape(bs, 1, D_MODEL)
    return (y_prompt, y_sample,
            jnp.stack(ckv_p), jnp.stack(kpe_p), jnp.stack(ckv_s), jnp.stack(kpe_s),
            jnp.stack(conv_p), jnp.stack(conv_s), jnp.stack(gla_p), jnp.stack(gla_s))
```
